```python
import math
import jax, jax.numpy as jnp
from jax import lax
import numpy as np

D_MODEL = 1024
BATCH = 8
SEQ = 4096
DEPTH = 2

MLA_HEADS = 8
MLA_Q_LORA = 256
MLA_KV_LORA = 256
MLA_NOPE = 64
MLA_ROPE = 32
MLA_V = 64
MLA_SCALE = (MLA_NOPE + MLA_ROPE) ** -0.5
ROPE_BASE = 10000.0
Q_BLOCK = 128
MAX_POS_OFFSET = 1024

SGU_GROUPS = 4
SGU_GROUP_DIM = 128
SGU_DIM = SGU_GROUPS * SGU_GROUP_DIM
SGU_CHUNK = 128

EVEN_IN = MLA_Q_LORA + MLA_KV_LORA + MLA_ROPE + 2 * SGU_DIM
EVEN_SPLITS = (MLA_Q_LORA,
               MLA_Q_LORA + MLA_KV_LORA,
               MLA_Q_LORA + MLA_KV_LORA + MLA_ROPE,
               MLA_Q_LORA + MLA_KV_LORA + MLA_ROPE + SGU_DIM)
EVEN_MIX = MLA_HEADS * MLA_V + SGU_DIM

HG_HEADS = 8
HG_DK = 128
HG_DV = D_MODEL // HG_HEADS
HG_KEY_DIM = HG_HEADS * HG_DK
HG_VAL_DIM = HG_HEADS * HG_DV
HG_CHUNK = 64
ODD_IN = 2 * HG_KEY_DIM + 2 * HG_VAL_DIM
ODD_SPLITS = (HG_KEY_DIM, 2 * HG_KEY_DIM, 2 * HG_KEY_DIM + HG_VAL_DIM)

D_FF = 4 * D_MODEL

N_EVEN = (DEPTH + 1) // 2
N_ODD = DEPTH // 2
DN_ALPHA = (2 * DEPTH) ** 0.25
DN_BETA = (8 * DEPTH) ** -0.25
NORM_EPS = 1e-5

kernel_name = 'hybrid_mla_sgu_hgrn2_deepnorm'


def layer_norm(x, g, b):
    xf = x.astype(jnp.float32)
    mu = jnp.mean(xf, -1, keepdims=True)
    var = jnp.mean(jnp.square(xf - mu), -1, keepdims=True)
    y = (xf - mu) * lax.rsqrt(var + NORM_EPS)
    return (y * g.astype(jnp.float32) + b.astype(jnp.float32)).astype(x.dtype)


def rms_norm(x, g):
    xf = x.astype(jnp.float32)
    y = xf * lax.rsqrt(jnp.mean(jnp.square(xf), -1, keepdims=True) + NORM_EPS)
    return (y * g.astype(jnp.float32)).astype(x.dtype)


def apply_rope(x, cos, sin):
    half = MLA_ROPE // 2
    xf = x.astype(jnp.float32)
    x1, x2 = xf[..., :half], xf[..., half:]
    out = jnp.concatenate([x1 * cos - x2 * sin, x2 * cos + x1 * sin], axis=-1)
    return out.astype(x.dtype)


def mla(c_q, c_kv, k_rope, positions, g_q, g_kv, w_qb, w_kvb):
    B, S, _ = c_q.shape
    q = (rms_norm(c_q, g_q) @ w_qb).reshape(B, S, MLA_HEADS, MLA_NOPE + MLA_ROPE)
    q_nope, q_rope = q[..., :MLA_NOPE], q[..., MLA_NOPE:]
    kv = (rms_norm(c_kv, g_kv) @ w_kvb).reshape(B, S, MLA_HEADS, MLA_NOPE + MLA_V)
    k_nope, v = kv[..., :MLA_NOPE], kv[..., MLA_NOPE:]
    half = MLA_ROPE // 2
    inv_freq = ROPE_BASE ** (-jnp.arange(half, dtype=jnp.float32) / half)
    ang = positions.astype(jnp.float32)[..., None] * inv_freq
    cos, sin = jnp.cos(ang), jnp.sin(ang)
    q_rope = apply_rope(q_rope, cos[:, :, None, :], sin[:, :, None, :])
    k_rope = apply_rope(k_rope, cos, sin)
    nb = S // Q_BLOCK
    qn_b = q_nope.reshape(B, nb, Q_BLOCK, MLA_HEADS, MLA_NOPE).transpose(1, 0, 2, 3, 4)
    qr_b = q_rope.reshape(B, nb, Q_BLOCK, MLA_HEADS, MLA_ROPE).transpose(1, 0, 2, 3, 4)
    key_idx = jnp.arange(S)

    def block(args):
        qn, qr, bi = args
        s = (jnp.einsum('bqhd,bkhd->bhqk', qn, k_nope)
             + jnp.einsum('bqhr,bkr->bhqk', qr, k_rope)).astype(jnp.float32) * MLA_SCALE
        q_idx = bi * Q_BLOCK + jnp.arange(Q_BLOCK)
        mask = key_idx[None, :] <= q_idx[:, None]
        s = jnp.where(mask[None, None], s, -jnp.inf)
        p = jax.nn.softmax(s, axis=-1).astype(v.dtype)
        return jnp.einsum('bhqk,bkhd->bqhd', p, v)

    out = lax.map(block, (qn_b, qr_b, jnp.arange(nb)))
    return out.transpose(1, 0, 2, 3, 4).reshape(B, S, MLA_HEADS * MLA_V)


def sgu(u, v, ln_g, ln_b, w_s, b_s):
    B, S, _ = u.shape
    u = jax.nn.gelu(u)
    v = layer_norm(jax.nn.gelu(v), ln_g, ln_b)
    nc = S // SGU_CHUNK
    vc = v.reshape(B, nc, SGU_CHUNK, SGU_GROUPS, SGU_GROUP_DIM)
    causal = jnp.tril(jnp.ones((SGU_CHUNK, SGU_CHUNK), dtype=bool))
    w = jnp.where(causal[None], w_s, jnp.zeros_like(w_s))
    mixed = jnp.einsum('gts,bnsgc->bntgc', w, vc) + b_s.T[:, :, None]
    return u * mixed.reshape(B, S, SGU_DIM)


def hgrn2(q, f, i, g, lb, g_norm):
    B, S, _ = q.shape
    nc = S // HG_CHUNK
    qf = jax.nn.silu(q.astype(jnp.float32))
    gate = lb + (1.0 - lb) * jax.nn.sigmoid(f.astype(jnp.float32))
    k = 1.0 - gate
    log_g = jnp.log(gate)
    vf = i.astype(jnp.float32)

    def chunks(t, d):
        return t.reshape(B, nc, HG_CHUNK, HG_HEADS, d).transpose(1, 0, 3, 2, 4)

    xs = (chunks(qf, HG_DK), chunks(k, HG_DK), chunks(vf, HG_DV), chunks(log_g, HG_DK))
    tri = jnp.tril(jnp.ones((HG_CHUNK, HG_CHUNK), dtype=bool))[:, :, None]

    def step(state, inp):
        qc, kc, vc, lg = inp
        bcum = jnp.cumsum(lg, axis=2)
        diff = bcum[:, :, :, None, :] - bcum[:, :, None, :, :]
        decay = jnp.exp(jnp.where(tri, diff, -jnp.inf))
        attn = jnp.einsum('bhtd,bhsd,bhtsd->bhts', qc, kc, decay)
        o = (jnp.einsum('bhts,bhsv->bhtv', attn, vc)
             + jnp.einsum('bhtd,bhdv->bhtv', qc * jnp.exp(bcum), state))
        b_last = bcum[:, :, -1:, :]
        k_dec = kc * jnp.exp(b_last - bcum)
        new_state = (jnp.exp(b_last[:, :, 0, :])[..., None] * state
                     + jnp.einsum('bhsd,bhsv->bhdv', k_dec, vc))
        return new_state, o

    state0 = jnp.zeros((B, HG_HEADS, HG_DK, HG_DV), jnp.float32)
    _, o = lax.scan(step, state0, xs)
    o = o.transpose(1, 0, 3, 2, 4).reshape(B, S, HG_HEADS, HG_DV)
    o = o * lax.rsqrt(jnp.mean(jnp.square(o), -1, keepdims=True) + NORM_EPS)
    o = o * g_norm.astype(jnp.float32).reshape(HG_HEADS, HG_DV)
    o = o * jax.nn.silu(g.astype(jnp.float32).reshape(B, S, HG_HEADS, HG_DV))
    return o.reshape(B, S, HG_VAL_DIM).astype(i.dtype)


def setup_inputs(seed: int = 0) -> dict:
    key = jax.random.key(seed)
    ks = jax.random.split(key, 24)

    def nrm(k, shape, scale):
        return jax.random.normal(k, shape, jnp.float32) * scale

    def gain(k, shape):
        return 1.0 + 0.01 * jax.random.normal(k, shape, jnp.float32)

    x = jax.random.normal(ks[0], (BATCH, SEQ, D_MODEL), jnp.float32)
    offs = jax.random.randint(ks[1], (BATCH, 1), 0, MAX_POS_OFFSET, dtype=jnp.int32)
    positions = (offs + jnp.arange(SEQ, dtype=jnp.int32)[None, :]).astype(jnp.int32)
    return {
        'x': x,
        'positions': positions,
        'w_in_e': nrm(ks[2], (N_EVEN, D_MODEL, EVEN_IN), D_MODEL ** -0.5),
        'mla_gq': gain(ks[3], (N_EVEN, MLA_Q_LORA)),
        'mla_gkv': gain(ks[4], (N_EVEN, MLA_KV_LORA)),
        'w_qb': nrm(ks[5], (N_EVEN, MLA_Q_LORA, MLA_HEADS * (MLA_NOPE + MLA_ROPE)), MLA_Q_LORA ** -0.5),
        'w_kvb': nrm(ks[6], (N_EVEN, MLA_KV_LORA, MLA_HEADS * (MLA_NOPE + MLA_V)), MLA_KV_LORA ** -0.5),
        'sgu_ln_g': gain(ks[7], (N_EVEN, SGU_DIM)),
        'sgu_ln_b': nrm(ks[8], (N_EVEN, SGU_DIM), 0.01),
        'sgu_w': nrm(ks[9], (N_EVEN, SGU_GROUPS, SGU_CHUNK, SGU_CHUNK), SGU_CHUNK ** -0.5),
        'sgu_b': gain(ks[10], (N_EVEN, SGU_GROUPS, SGU_CHUNK)),
        'w_out_e': nrm(ks[11], (N_EVEN, EVEN_MIX, D_MODEL), DN_BETA * EVEN_MIX ** -0.5),
        'w_in_o': nrm(ks[12], (N_ODD, D_MODEL, ODD_IN), D_MODEL ** -0.5),
        'hg_lb': nrm(ks[13], (DEPTH, HG_KEY_DIM), 0.1),
        'hg_gnorm': gain(ks[14], (N_ODD, HG_VAL_DIM)),
        'w_out_o': nrm(ks[15], (N_ODD, HG_VAL_DIM, D_MODEL), DN_BETA * HG_VAL_DIM ** -0.5),
        'ln1_g': gain(ks[16], (DEPTH, D_MODEL)),
        'ln1_b': nrm(ks[17], (DEPTH, D_MODEL), 0.01),
        'w_ff1': nrm(ks[18], (DEPTH, D_MODEL, D_FF), DN_BETA * D_MODEL ** -0.5),
        'w_ff2': nrm(ks[19], (DEPTH, D_FF, D_MODEL), DN_BETA * D_FF ** -0.5),
        'ln2_g': gain(ks[20], (DEPTH, D_MODEL)),
        'ln2_b': nrm(ks[21], (DEPTH, D_MODEL), 0.01),
    }


def reference(x, positions, w_in_e, mla_gq, mla_gkv, w_qb, w_kvb, sgu_ln_g, sgu_ln_b,
              sgu_w, sgu_b, w_out_e, w_in_o, hg_lb, hg_gnorm, w_out_o,
              ln1_g, ln1_b, w_ff1, w_ff2, ln2_g, ln2_b):
    lb_sm = jax.nn.softmax(hg_lb.astype(jnp.float32), axis=0)
    lb_all = jnp.cumsum(lb_sm, axis=0) - lb_sm[0:1]
    h = x
    for l in range(DEPTH):
        if l % 2 == 0:
            e = l // 2
            z = h @ w_in_e[e]
            c_q, c_kv, k_r, u, v = jnp.split(z, EVEN_SPLITS, axis=-1)
            a_out = mla(c_q, c_kv, k_r, positions, mla_gq[e], mla_gkv[e], w_qb[e], w_kvb[e])
            b_out = sgu(u, v, sgu_ln_g[e], sgu_ln_b[e], sgu_w[e], sgu_b[e])
            mix = jnp.concatenate([a_out, b_out], axis=-1) @ w_out_e[e]
        else:
            o = l // 2
            z = h @ w_in_o[o]
            q, f, i, g = jnp.split(z, ODD_SPLITS, axis=-1)
            mix = hgrn2(q, f, i, g, lb_all[l], hg_gnorm[o]) @ w_out_o[o]
        h = layer_norm(DN_ALPHA * h + mix, ln1_g[l], ln1_b[l])
        ff = jnp.square(jax.nn.relu(h @ w_ff1[l])) @ w_ff2[l]
        h = layer_norm(DN_ALPHA * h + ff, ln2_g[l], ln2_b[l])
    return h
```

```python
import functools
import math

import jax
import jax.numpy as jnp
from jax import lax
from jax.experimental import pallas as pl
from jax.experimental.pallas import tpu as pltpu

D_MODEL = 1024
DEPTH = 2
MLA_HEADS = 8
MLA_LORA = 256
MLA_NOPE = 64
MLA_ROPE = 32
MLA_V = 64
MLA_SCALE = (MLA_NOPE + MLA_ROPE) ** -0.5
ROPE_BASE = 10000.0
SGU_GROUPS = 4
SGU_GROUP_DIM = 128
SGU_DIM = SGU_GROUPS * SGU_GROUP_DIM
SGU_CHUNK = 128
HG_HEADS = 8
HG_DK = 128
HG_DV = 128
D_FF = 4 * D_MODEL
DN_ALPHA = (2 * DEPTH) ** 0.25
NORM_EPS = 1e-5

LANES = 128
SUBLANES = 8
VMEM_LIMIT_BYTES = 56 * 1024 * 1024

TOKEN_TILE = 512
ATTN_TILE = 512
HG_CHUNK = 128
HG_BLOCK = 512
FF_CHUNK = 1024
MASK_VALUE = -1e30

HEAD_PAD = LANES
COL_CQ = 0
COL_CKV = COL_CQ + MLA_LORA
COL_KR = COL_CKV + MLA_LORA
COL_U = COL_KR + HEAD_PAD
COL_V = COL_U + SGU_DIM
EVEN_COLS = COL_V + SGU_DIM
ROPE_HALF = MLA_ROPE // 2
ROPE_LO = MLA_NOPE


def _params(semantics):
    return pltpu.CompilerParams(dimension_semantics=semantics, vmem_limit_bytes=VMEM_LIMIT_BYTES)


def _resident(shape):
    zeros = (0,) * len(shape)
    return pl.BlockSpec(shape, lambda *_: zeros, pipeline_mode=pl.Buffered(1))


def _dot(a, b):
    return jnp.dot(a, b, preferred_element_type=jnp.float32)


def _dot_nt(a, b):
    return lax.dot_general(a, b, (((1,), (1,)), ((), ())), preferred_element_type=jnp.float32)


def _sigmoid(x):
    return 1.0 / (1.0 + jnp.exp(-x))


def _gelu_tanh(x):
    c = math.sqrt(2.0 / math.pi)
    return 0.5 * x * (1.0 + jnp.tanh(c * (x + 0.044715 * (x * x * x))))


def _layer_norm(x, g, b):
    mu = jnp.mean(x, axis=-1, keepdims=True)
    xc = x - mu
    var = jnp.mean(xc * xc, axis=-1, keepdims=True)
    return xc * lax.rsqrt(var + NORM_EPS) * g + b


def _rms_norm(x, g):
    return x * lax.rsqrt(jnp.mean(x * x, axis=-1, keepdims=True) + NORM_EPS) * g


def _even_front_kernel(x_ref, pos_ref, freq_ref, sign_ref, w_in_ref, gq_ref, gkv_ref, wq_ref, wk_ref, wv_ref,
                       lng_ref, lnb_ref, sw_ref, sb_ref, q_ref, k_ref, v_ref, b_ref):
    tm = x_ref.shape[0]
    xb = x_ref[...].astype(jnp.bfloat16)
    z = _dot(xb, w_in_ref[...])

    ang = pos_ref[...].astype(jnp.float32) * freq_ref[...]
    cos_t = jnp.cos(ang)
    sin_t = jnp.sin(ang) * sign_ref[...]
    lane = lax.broadcasted_iota(jnp.int32, (tm, LANES), 1)
    lower = lane < (ROPE_LO + ROPE_HALF)

    def rope(slab, c, s):
        partner = jnp.where(lower, pltpu.roll(slab, LANES - ROPE_HALF, 1), pltpu.roll(slab, ROPE_HALF, 1))
        return slab * c + partner * s

    cq = _rms_norm(z[:, COL_CQ:COL_CQ + MLA_LORA], gq_ref[...]).astype(jnp.bfloat16)
    ckv = _rms_norm(z[:, COL_CKV:COL_CKV + MLA_LORA], gkv_ref[...]).astype(jnp.bfloat16)
    q = _dot(cq, wq_ref[...])
    kn = _dot(ckv, wk_ref[...])
    v_ref[...] = _dot(ckv, wv_ref[...]).astype(v_ref.dtype)
    kr = rope(z[:, COL_KR:COL_KR + HEAD_PAD], cos_t, sin_t)
    cos_q = cos_t * MLA_SCALE
    sin_q = sin_t * MLA_SCALE
    for h in range(MLA_HEADS):
        sl = slice(h * HEAD_PAD, (h + 1) * HEAD_PAD)
        q_ref[:, sl] = rope(q[:, sl], cos_q, sin_q).astype(q_ref.dtype)
        k_ref[:, sl] = (kn[:, sl] + kr).astype(k_ref.dtype)

    u = _gelu_tanh(z[:, COL_U:COL_U + SGU_DIM])
    vn = _layer_norm(_gelu_tanh(z[:, COL_V:COL_V + SGU_DIM]), lng_ref[...], lnb_ref[...]).astype(jnp.bfloat16)
    row = lax.broadcasted_iota(jnp.int32, (SGU_CHUNK, SGU_CHUNK), 0)
    col = lax.broadcasted_iota(jnp.int32, (SGU_CHUNK, SGU_CHUNK), 1)
    causal = col <= row
    bias = sb_ref[...]
    for g in range(SGU_GROUPS):
        wg = jnp.where(causal, sw_ref[g], 0.0).astype(jnp.bfloat16)
        gs = slice(g * SGU_GROUP_DIM, (g + 1) * SGU_GROUP_DIM)
        for c in range(tm // SGU_CHUNK):
            rs = slice(c * SGU_CHUNK, (c + 1) * SGU_CHUNK)
            mixed = _dot(wg, vn[rs, gs]) + bias[:, gs]
            b_ref[rs, gs] = (u[rs, gs] * mixed).astype(b_ref.dtype)


def _even_front(x2, pos2, freq_row, sign_row, w_in, gq, gkv, wq, wk, wv, lng, lnb, sw, sb):
    t = x2.shape[0]
    tm = TOKEN_TILE
    row = lambda i: (i, 0)
    out_shapes = (
        jax.ShapeDtypeStruct((t, MLA_HEADS * HEAD_PAD), jnp.bfloat16),
        jax.ShapeDtypeStruct((t, MLA_HEADS * HEAD_PAD), jnp.bfloat16),
        jax.ShapeDtypeStruct((t, MLA_HEADS * MLA_V), jnp.bfloat16),
        jax.ShapeDtypeStruct((t, SGU_DIM), jnp.bfloat16),
    )
    return pl.pallas_call(
        _even_front_kernel,
        grid=(t // tm,),
        in_specs=[
            pl.BlockSpec((tm, D_MODEL), row),
            pl.BlockSpec((tm, 1), row),
            _resident(freq_row.shape), _resident(sign_row.shape), _resident(w_in.shape),
            _resident(gq.shape), _resident(gkv.shape), _resident(wq.shape), _resident(wk.shape),
            _resident(wv.shape), _resident(lng.shape), _resident(lnb.shape), _resident(sw.shape),
            _resident(sb.shape),
        ],
        out_specs=(
            pl.BlockSpec((tm, MLA_HEADS * HEAD_PAD), row),
            pl.BlockSpec((tm, MLA_HEADS * HEAD_PAD), row),
            pl.BlockSpec((tm, MLA_HEADS * MLA_V), row),
            pl.BlockSpec((tm, SGU_DIM), row),
        ),
        out_shape=out_shapes,
        compiler_params=_params(("parallel",)),
        name="even_front",
    )(x2, pos2, freq_row, sign_row, w_in, gq, gkv, wq, wk, wv, lng, lnb, sw, sb)


def _attn_kernel(q_ref, k_ref, v_ref, o_ref, m_sc, l_sc, acc_sc):
    tq = q_ref.shape[0]
    tk = ATTN_TILE
    qi = pl.program_id(2)
    rep = tk // LANES
    row = lax.broadcasted_iota(jnp.int32, (tq, tk), 0)
    col = lax.broadcasted_iota(jnp.int32, (tq, tk), 1)
    diag_ok = col <= row
    outs = []
    for hh in range(2):
        hs = slice(hh * HEAD_PAD, (hh + 1) * HEAD_PAD)
        q = q_ref[:, hs]
        m_sc[...] = jnp.full(m_sc.shape, MASK_VALUE, jnp.float32)
        l_sc[...] = jnp.zeros(l_sc.shape, jnp.float32)
        acc_sc[...] = jnp.zeros(acc_sc.shape, jnp.float32)

        def step(j, masked):
            start = pl.multiple_of(j * tk, tk)
            s = _dot_nt(q, k_ref[pl.ds(start, tk), hs])
            if masked:
                s = jnp.where(diag_ok, s, MASK_VALUE)
            m_prev = m_sc[...]
            m_next = jnp.maximum(m_prev, jnp.max(s, axis=1, keepdims=True))
            p = jnp.exp(s - pltpu.repeat(m_next, rep, 1))
            alpha = jnp.exp(m_prev - m_next)
            l_sc[...] = alpha * l_sc[...] + jnp.sum(p, axis=1, keepdims=True)
            acc_sc[...] = alpha * acc_sc[...] + _dot(p.astype(jnp.bfloat16), v_ref[pl.ds(start, tk), :])
            m_sc[...] = m_next

        def body(j, carry):
            step(j, False)
            return carry

        lax.fori_loop(0, qi, body, 0)
        step(qi, True)
        outs.append(acc_sc[...] / l_sc[...])
    lane = lax.broadcasted_iota(jnp.int32, (tq, LANES), 1)
    o_ref[...] = jnp.where(lane < MLA_V, outs[0], outs[1]).astype(o_ref.dtype)


def _attention(q, k, v):
    b, s, _ = q.shape
    tq = ATTN_TILE
    pairs = MLA_HEADS // 2
    return pl.pallas_call(
        _attn_kernel,
        grid=(b, pairs, s // tq),
        in_specs=[
            pl.BlockSpec((None, tq, 2 * HEAD_PAD), lambda bi, p, qi: (bi, qi, p)),
            pl.BlockSpec((None, s, 2 * HEAD_PAD), lambda bi, p, qi: (bi, 0, p)),
            pl.BlockSpec((None, s, 2 * MLA_V), lambda bi, p, qi: (bi, 0, p)),
        ],
        out_specs=pl.BlockSpec((None, tq, 2 * MLA_V), lambda bi, p, qi: (bi, qi, p)),
        out_shape=jax.ShapeDtypeStruct((b, s, MLA_HEADS * MLA_V), jnp.bfloat16),
        scratch_shapes=[
            pltpu.VMEM((tq, LANES), jnp.float32),
            pltpu.VMEM((tq, LANES), jnp.float32),
            pltpu.VMEM((tq, LANES), jnp.float32),
        ],
        compiler_params=_params(("parallel", "parallel", "arbitrary")),
        name="mla_attention",
    )(q, k, v)


def _post_kernel(n_mix, *refs):
    h_ref = refs[0]
    mix_refs = refs[1:1 + n_mix]
    wout_refs = refs[1 + n_mix:1 + 2 * n_mix]
    g1_ref, b1_ref, w1_ref, w2_ref, g2_ref, b2_ref, o_ref = refs[1 + 2 * n_mix:]
    mix = _dot(mix_refs[0][...], wout_refs[0][...])
    for m_ref, w_ref in zip(mix_refs[1:], wout_refs[1:]):
        mix = mix + _dot(m_ref[...], w_ref[...])
    y = _layer_norm(DN_ALPHA * h_ref[...] + mix, g1_ref[...], b1_ref[...])
    yb = y.astype(jnp.bfloat16)
    ff = None
    for c in range(D_FF // FF_CHUNK):
        cs = slice(c * FF_CHUNK, (c + 1) * FF_CHUNK)
        a = jnp.maximum(_dot(yb, w1_ref[:, cs]), 0.0)
        part = _dot((a * a).astype(jnp.bfloat16), w2_ref[cs, :])
        ff = part if ff is None else ff + part
    o_ref[...] = _layer_norm(DN_ALPHA * y + ff, g2_ref[...], b2_ref[...])


def _post(h2, mixes, wouts, g1, b1, w1, w2, g2, b2):
    t = h2.shape[0]
    tm = TOKEN_TILE
    row = lambda i: (i, 0)
    n = len(mixes)
    in_specs = [pl.BlockSpec((tm, D_MODEL), row)]
    in_specs += [pl.BlockSpec((tm, m.shape[1]), row) for m in mixes]
    in_specs += [_resident(w.shape) for w in wouts]
    in_specs += [_resident(a.shape) for a in (g1, b1, w1, w2, g2, b2)]
    return pl.pallas_call(
        functools.partial(_post_kernel, n),
        grid=(t // tm,),
        in_specs=in_specs,
        out_specs=pl.BlockSpec((tm, D_MODEL), row),
        out_shape=jax.ShapeDtypeStruct((t, D_MODEL), jnp.float32),
        compiler_params=_params(("parallel",)),
        name="post",
    )(h2, *mixes, *wouts, g1, b1, w1, w2, g2, b2)


def _matmul_kernel(x_ref, w_ref, o_ref):
    xb = x_ref[...].astype(jnp.bfloat16)
    n = w_ref.shape[1]
    for c in range(n // FF_CHUNK):
        cs = slice(c * FF_CHUNK, (c + 1) * FF_CHUNK)
        o_ref[:, cs] = _dot(xb, w_ref[:, cs]).astype(o_ref.dtype)


def _matmul(x2, w):
    t = x2.shape[0]
    tm = TOKEN_TILE
    n = w.shape[1]
    return pl.pallas_call(
        _matmul_kernel,
        grid=(t // tm,),
        in_specs=[pl.BlockSpec((tm, x2.shape[1]), lambda i: (i, 0)), _resident(w.shape)],
        out_specs=pl.BlockSpec((tm, n), lambda i: (i, 0)),
        out_shape=jax.ShapeDtypeStruct((t, n), jnp.bfloat16),
        compiler_params=_params(("parallel",)),
        name="odd_in_proj",
    )(x2, w)


def _midpoint_rows(b, half):
    c = b.shape[0]
    if half >= SUBLANES:
        parts = []
        for p in range(c // (2 * half)):
            r = p * 2 * half + half - 1
            parts.append(jnp.broadcast_to(b[r:r + 1, :], (2 * half, LANES)))
        return parts[0] if len(parts) == 1 else jnp.concatenate(parts, axis=0)
    b3 = b.reshape(c // SUBLANES, SUBLANES, LANES)
    sub = lax.broadcasted_iota(jnp.int32, b3.shape, 1)
    out = None
    for p in range(SUBLANES // (2 * half)):
        r = p * 2 * half + half - 1
        cand = jnp.broadcast_to(b3[:, r:r + 1, :], b3.shape)
        out = cand if out is None else jnp.where(sub >= p * 2 * half, cand, out)
    return out.reshape(c, LANES)


def _hgrn_kernel(z_ref, lb_ref, gn_ref, tril_ref, o_ref, st_ref):
    c = HG_CHUNK
    n_levels = int(math.log2(c))

    @pl.when(pl.program_id(2) == 0)
    def _():
        st_ref[...] = jnp.zeros(st_ref.shape, jnp.float32)

    lbp = lb_ref[...]
    mx = jnp.max(lbp, axis=0, keepdims=True)
    e = jnp.exp(lbp - mx)
    sm = e / jnp.sum(e, axis=0, keepdims=True)
    lb = (sm[0:1, :] + sm[1:2, :]) - sm[0:1, :]
    gn = gn_ref[...]
    tril = tril_ref[...]

    t_idx = lax.broadcasted_iota(jnp.int32, (c, c), 0)
    s_idx = lax.broadcasted_iota(jnp.int32, (c, c), 1)
    x = jnp.bitwise_xor(t_idx, s_idx)
    level = jnp.zeros((c, c), jnp.int32)
    for bit in range(n_levels):
        level = level + (x >= (1 << bit)).astype(jnp.int32)
    level = jnp.where(s_idx <= t_idx, level, -1)
    row_id = lax.broadcasted_iota(jnp.int32, (c, LANES), 0)

    for ci in range(z_ref.shape[0] // c):
        rs = slice(ci * c, (ci + 1) * c)
        qf = z_ref[rs, 0 * HG_DK:1 * HG_DK].astype(jnp.float32)
        ff = z_ref[rs, 1 * HG_DK:2 * HG_DK].astype(jnp.float32)
        vb = z_ref[rs, 2 * HG_DK:3 * HG_DK]
        gf = z_ref[rs, 3 * HG_DK:4 * HG_DK].astype(jnp.float32)
        q = qf * _sigmoid(qf)
        gate = lb + (1.0 - lb) * _sigmoid(ff)
        k = 1.0 - gate
        lg = jnp.log(gate)
        b = jnp.dot(tril, lg, precision=lax.Precision.HIGHEST, preferred_element_type=jnp.float32)
        b_last = b[c - 1:c, :]
        st = st_ref[...]

        o = _dot_nt((q * jnp.exp(b)).astype(jnp.bfloat16), st.astype(jnp.bfloat16))
        qb = q.astype(jnp.bfloat16)
        kb = k.astype(jnp.bfloat16)
        attn = jnp.where(level == 0, _dot_nt(qb, kb), 0.0)
        for lv in range(1, n_levels + 1):
            half = 1 << (lv - 1)
            right = jnp.bitwise_and(row_id, half) != 0
            d = b - _midpoint_rows(b, half)
            decay = jnp.exp(jnp.where(right, d, -d))
            zl = (jnp.where(right, q, k) * decay).astype(jnp.bfloat16)
            attn = jnp.where(level == lv, _dot_nt(zl, zl), attn)
        o = o + _dot(attn.astype(jnp.bfloat16), vb)

        kd = (k * jnp.exp(b_last - b)).astype(jnp.bfloat16)
        upd = lax.dot_general(vb, kd, (((0,), (0,)), ((), ())), preferred_element_type=jnp.float32)
        st_ref[...] = st * jnp.exp(b_last) + upd

        o = o * lax.rsqrt(jnp.mean(o * o, axis=-1, keepdims=True) + NORM_EPS)
        o_ref[rs, :] = (o * gn * (gf * _sigmoid(gf))).astype(o_ref.dtype)


def _hgrn(z, lb, gn, tril):
    b, s, _ = z.shape
    blk = HG_BLOCK
    return pl.pallas_call(
        _hgrn_kernel,
        grid=(b, HG_HEADS, s // blk),
        in_specs=[
            pl.BlockSpec((None, blk, 4 * HG_DK), lambda bi, h, li: (bi, li, h)),
            pl.BlockSpec((DEPTH, HG_DK), lambda bi, h, li: (0, h)),
            pl.BlockSpec((1, HG_DV), lambda bi, h, li: (0, h)),
            _resident(tril.shape),
        ],
        out_specs=pl.BlockSpec((None, blk, HG_DV), lambda bi, h, li: (bi, li, h)),
        out_shape=jax.ShapeDtypeStruct((b, s, HG_HEADS * HG_DV), jnp.bfloat16),
        scratch_shapes=[pltpu.VMEM((HG_DV, HG_DK), jnp.float32)],
        compiler_params=_params(("parallel", "parallel", "arbitrary")),
        name="hgrn2",
    )(z, lb, gn, tril)


def kernel(x, positions, w_in_e, mla_gq, mla_gkv, w_qb, w_kvb, sgu_ln_g, sgu_ln_b, sgu_w, sgu_b, w_out_e,
           w_in_o, hg_lb, hg_gnorm, w_out_o, ln1_g, ln1_b, w_ff1, w_ff2, ln2_g, ln2_b):
    bsz, seq, d = x.shape
    assert d == D_MODEL and hg_lb.shape[0] == DEPTH == 2
    assert seq % ATTN_TILE == 0 and seq % HG_BLOCK == 0 and (bsz * seq) % TOKEN_TILE == 0
    t = bsz * seq
    bf = jnp.bfloat16
    f32 = jnp.float32
    row = lambda a: a.reshape(1, -1).astype(f32)

    inv_freq = ROPE_BASE ** (-jnp.arange(ROPE_HALF, dtype=f32) / ROPE_HALF)
    freq_row = jnp.zeros((LANES,), f32).at[ROPE_LO:ROPE_LO + ROPE_HALF].set(inv_freq)
    freq_row = freq_row.at[ROPE_LO + ROPE_HALF:ROPE_LO + MLA_ROPE].set(inv_freq).reshape(1, LANES)
    sign_row = jnp.zeros((LANES,), f32).at[ROPE_LO:ROPE_LO + ROPE_HALF].set(-1.0)
    sign_row = sign_row.at[ROPE_LO + ROPE_HALF:ROPE_LO + MLA_ROPE].set(1.0).reshape(1, LANES)

    we = w_in_e[0]
    s0, s1, s2, s3 = MLA_LORA, 2 * MLA_LORA, 2 * MLA_LORA + MLA_ROPE, 2 * MLA_LORA + MLA_ROPE + SGU_DIM
    kr_cols = jnp.pad(we[:, s1:s2], ((0, 0), (ROPE_LO, HEAD_PAD - ROPE_LO - MLA_ROPE)))
    w_in_p = jnp.concatenate([we[:, :s1], kr_cols, we[:, s2:s3], we[:, s3:]], axis=1).astype(bf)
    assert w_in_p.shape[1] == EVEN_COLS
    dq = MLA_NOPE + MLA_ROPE
    wq_p = jnp.pad(w_qb[0].reshape(MLA_LORA, MLA_HEADS, dq), ((0, 0), (0, 0), (0, HEAD_PAD - dq)))
    wq_p = wq_p.reshape(MLA_LORA, MLA_HEADS * HEAD_PAD).astype(bf)
    wkv = w_kvb[0].reshape(MLA_LORA, MLA_HEADS, MLA_NOPE + MLA_V)
    wk_p = jnp.pad(wkv[:, :, :MLA_NOPE], ((0, 0), (0, 0), (0, HEAD_PAD - MLA_NOPE)))
    wk_p = wk_p.reshape(MLA_LORA, MLA_HEADS * HEAD_PAD).astype(bf)
    wv_p = wkv[:, :, MLA_NOPE:].reshape(MLA_LORA, MLA_HEADS * MLA_V).astype(bf)
    sgu_bias = jnp.repeat(sgu_b[0].T, SGU_GROUP_DIM, axis=1).astype(f32)

    x2 = x.reshape(t, D_MODEL)
    pos2 = positions.reshape(t, 1)
    q, k, v, b_out = _even_front(x2, pos2, freq_row, sign_row, w_in_p, row(mla_gq[0]), row(mla_gkv[0]), wq_p, wk_p,
                                 wv_p, row(sgu_ln_g[0]), row(sgu_ln_b[0]), sgu_w[0].astype(f32), sgu_bias)
    a_out = _attention(q.reshape(bsz, seq, -1), k.reshape(bsz, seq, -1), v.reshape(bsz, seq, -1))
    a_rows = MLA_HEADS * MLA_V
    h1 = _post(x2, [a_out.reshape(t, -1), b_out], [w_out_e[0][:a_rows].astype(bf), w_out_e[0][a_rows:].astype(bf)],
               row(ln1_g[0]), row(ln1_b[0]), w_ff1[0].astype(bf), w_ff2[0].astype(bf), row(ln2_g[0]), row(ln2_b[0]))

    wo = w_in_o[0].reshape(D_MODEL, 4, HG_HEADS, HG_DK).transpose(0, 2, 1, 3).reshape(D_MODEL, 4 * HG_HEADS * HG_DK)
    z = _matmul(h1, wo.astype(bf))
    tril = jnp.tril(jnp.ones((HG_CHUNK, HG_CHUNK), f32))
    o = _hgrn(z.reshape(bsz, seq, -1), hg_lb.astype(f32), row(hg_gnorm[0]), tril)
    h2 = _post(h1, [o.reshape(t, -1)], [w_out_o[0].astype(bf)], row(ln1_g[1]), row(ln1_b[1]), w_ff1[1].astype(bf),
               w_ff2[1].astype(bf), row(ln2_g[1]), row(ln2_b[1]))
    return h2.reshape(bsz, seq, D_MODEL)
```

```python
import functools
import math

import jax
import jax.numpy as jnp
from jax import lax
from jax.experimental import pallas as pl
from jax.experimental.pallas import tpu as pltpu

D_MODEL = 1024
DEPTH = 2
MLA_HEADS = 8
MLA_LORA = 256
MLA_NOPE = 64
MLA_ROPE = 32
MLA_V = 64
MLA_SCALE = (MLA_NOPE + MLA_ROPE) ** -0.5
LOG2_E = math.log2(math.e)
ROPE_BASE = 10000.0
SGU_GROUPS = 4
SGU_GROUP_DIM = 128
SGU_DIM = SGU_GROUPS * SGU_GROUP_DIM
SGU_CHUNK = 128
HG_HEADS = 8
HG_DK = 128
HG_DV = 128
D_FF = 4 * D_MODEL
DN_ALPHA = (2 * DEPTH) ** 0.25
NORM_EPS = 1e-5

LANES = 128
SUBLANES = 8
VMEM_LIMIT_BYTES = 56 * 1024 * 1024

TOKEN_TILE = 512
ATTN_TILE = 512
HG_CHUNK = 128
HG_BLOCK = 512
FF_CHUNK = 1024
MASK_VALUE = -1e30

HEAD_PAD = LANES
COL_CQ = 0
COL_CKV = COL_CQ + MLA_LORA
COL_KR = COL_CKV + MLA_LORA
COL_U = COL_KR + HEAD_PAD
COL_V = COL_U + SGU_DIM
EVEN_COLS = COL_V + SGU_DIM
ROPE_HALF = MLA_ROPE // 2
ROPE_LO = MLA_NOPE


def _params(semantics):
    return pltpu.CompilerParams(dimension_semantics=semantics, vmem_limit_bytes=VMEM_LIMIT_BYTES)


def _resident(shape):
    zeros = (0,) * len(shape)
    return pl.BlockSpec(shape, lambda *_: zeros, pipeline_mode=pl.Buffered(1))


def _dot(a, b):
    return jnp.dot(a, b, preferred_element_type=jnp.float32)


def _dot_nt(a, b):
    return lax.dot_general(a, b, (((1,), (1,)), ((), ())), preferred_element_type=jnp.float32)


def _sigmoid(x):
    return 1.0 / (1.0 + jnp.exp(-x))


def _gelu_tanh(x):
    c = math.sqrt(2.0 / math.pi)
    return 0.5 * x * (1.0 + jnp.tanh(c * (x + 0.044715 * (x * x * x))))


def _layer_norm(x, g, b):
    mu = jnp.mean(x, axis=-1, keepdims=True)
    xc = x - mu
    var = jnp.mean(xc * xc, axis=-1, keepdims=True)
    return xc * lax.rsqrt(var + NORM_EPS) * g + b


def _rms_norm(x, g):
    return x * lax.rsqrt(jnp.mean(x * x, axis=-1, keepdims=True) + NORM_EPS) * g


def _even_front_kernel(x_ref, pos_ref, freq_ref, w_in_ref, gq_ref, gkv_ref, wqt_ref, wk_ref, wvt_ref, one_ref,
                       lng_ref, lnb_ref, sw_ref, sb_ref, qt_ref, k_ref, vt_ref, b_ref):
    tm = x_ref.shape[0]
    rep = tm // LANES
    xb = x_ref[...].astype(jnp.bfloat16)
    z = _dot(xb, w_in_ref[...])

    ang = pltpu.repeat(freq_ref[...], rep, 1) * pos_ref[...].astype(jnp.float32)
    cos_t = jnp.cos(ang)
    sin_t = jnp.sin(ang)

    cq = _rms_norm(z[:, COL_CQ:COL_CQ + MLA_LORA], gq_ref[...] * (MLA_SCALE * LOG2_E)).astype(jnp.bfloat16)
    ckv = _rms_norm(z[:, COL_CKV:COL_CKV + MLA_LORA], gkv_ref[...]).astype(jnp.bfloat16)

    qt = _dot_nt(wqt_ref[...], cq)
    for h in range(MLA_HEADS):
        base = h * HEAD_PAD
        x1 = qt[base + ROPE_LO:base + ROPE_LO + ROPE_HALF]
        x2 = qt[base + ROPE_LO + ROPE_HALF:base + ROPE_LO + MLA_ROPE]
        slab = jnp.concatenate([qt[base:base + ROPE_LO], x1 * cos_t - x2 * sin_t, x2 * cos_t + x1 * sin_t,
                                qt[base + ROPE_LO + MLA_ROPE:base + HEAD_PAD]], axis=0)
        qt_ref[base:base + HEAD_PAD, :] = slab.astype(qt_ref.dtype)

    vt = _dot_nt(wvt_ref[...], ckv) + pltpu.repeat(one_ref[...], rep, 1)
    vt_ref[...] = vt.astype(vt_ref.dtype)

    ones = jnp.ones((ROPE_LO, tm), jnp.float32)
    zeros = jnp.zeros((HEAD_PAD - ROPE_LO - MLA_ROPE, tm), jnp.float32)
    cos_k = jnp.concatenate([ones, cos_t, cos_t, zeros], axis=0).T
    sin_k = jnp.concatenate([0.0 * ones, -sin_t, sin_t, zeros], axis=0).T
    lane = lax.broadcasted_iota(jnp.int32, (tm, LANES), 1)
    kr = z[:, COL_KR:COL_KR + HEAD_PAD]
    partner = jnp.where(lane < ROPE_LO + ROPE_HALF, pltpu.roll(kr, LANES - ROPE_HALF, 1), pltpu.roll(kr, ROPE_HALF, 1))
    kr = kr * cos_k + partner * sin_k
    kn = _dot(ckv, wk_ref[...])
    for h in range(MLA_HEADS):
        sl = slice(h * HEAD_PAD, (h + 1) * HEAD_PAD)
        k_ref[:, sl] = (kn[:, sl] + kr).astype(k_ref.dtype)

    u = _gelu_tanh(z[:, COL_U:COL_U + SGU_DIM])
    vn = _layer_norm(_gelu_tanh(z[:, COL_V:COL_V + SGU_DIM]), lng_ref[...], lnb_ref[...]).astype(jnp.bfloat16)
    row = lax.broadcasted_iota(jnp.int32, (SGU_CHUNK, SGU_CHUNK), 0)
    col = lax.broadcasted_iota(jnp.int32, (SGU_CHUNK, SGU_CHUNK), 1)
    causal = col <= row
    bias = sb_ref[...]
    for g in range(SGU_GROUPS):
        wg = jnp.where(causal, sw_ref[g], 0.0).astype(jnp.bfloat16)
        gs = slice(g * SGU_GROUP_DIM, (g + 1) * SGU_GROUP_DIM)
        for c in range(tm // SGU_CHUNK):
            rs = slice(c * SGU_CHUNK, (c + 1) * SGU_CHUNK)
            mixed = _dot(wg, vn[rs, gs]) + bias[:, gs]
            b_ref[rs, gs] = (u[rs, gs] * mixed).astype(b_ref.dtype)


def _even_front(x2, pos_row, freq_col, w_in, gq, gkv, wqt, wk, wvt, one_col, lng, lnb, sw, sb):
    t = x2.shape[0]
    tm = TOKEN_TILE
    row = lambda i: (i, 0)
    col = lambda i: (0, i)
    wide = MLA_HEADS * HEAD_PAD
    out_shapes = (
        jax.ShapeDtypeStruct((wide, t), jnp.bfloat16),
        jax.ShapeDtypeStruct((t, wide), jnp.bfloat16),
        jax.ShapeDtypeStruct((wide, t), jnp.bfloat16),
        jax.ShapeDtypeStruct((t, SGU_DIM), jnp.bfloat16),
    )
    consts = (freq_col, w_in, gq, gkv, wqt, wk, wvt, one_col, lng, lnb, sw, sb)
    return pl.pallas_call(
        _even_front_kernel,
        grid=(t // tm,),
        in_specs=[pl.BlockSpec((tm, D_MODEL), row), pl.BlockSpec((1, tm), col)] + [_resident(a.shape) for a in consts],
        out_specs=(
            pl.BlockSpec((wide, tm), col),
            pl.BlockSpec((tm, wide), row),
            pl.BlockSpec((wide, tm), col),
            pl.BlockSpec((tm, SGU_DIM), row),
        ),
        out_shape=out_shapes,
        compiler_params=_params(("parallel",)),
        name="even_front",
    )(x2, pos_row, *consts)


def _attn_kernel(qt_ref, k_ref, vt_ref, o_ref, s_sc, cm_sc, m_sc, acc_sc):
    tq = qt_ref.shape[1]
    tk = ATTN_TILE
    qi = pl.program_id(2)
    m_sc[...] = jnp.full(m_sc.shape, MASK_VALUE, jnp.float32)
    acc_sc[...] = jnp.zeros(acc_sc.shape, jnp.float32)

    def scores(tile, slot, hh, masked):
        start = pl.multiple_of(tile * tk, tk)
        hs = slice(hh * HEAD_PAD, (hh + 1) * HEAD_PAD)
        s = _dot(k_ref[pl.ds(start, tk), hs], qt_ref[hs, :])
        if masked:
            key = lax.broadcasted_iota(jnp.int32, (tk, tq), 0)
            qry = lax.broadcasted_iota(jnp.int32, (tk, tq), 1)
            s = jnp.where(key <= qry, s, MASK_VALUE)
        s_sc[slot, hh] = s
        cm_sc[slot, hh] = jnp.broadcast_to(jnp.max(s, axis=0, keepdims=True), (SUBLANES, tq))

    def update(tile, slot, hh):
        start = pl.multiple_of(tile * tk, tk)
        hs = slice(hh * HEAD_PAD, (hh + 1) * HEAD_PAD)
        m_prev = m_sc[hh]
        m_next = jnp.maximum(m_prev, cm_sc[slot, hh])
        p = jnp.exp2(s_sc[slot, hh] - m_next[0:1, :]).astype(jnp.bfloat16)
        alpha = jnp.exp2(m_prev - m_next)
        pv = _dot(vt_ref[hs, pl.ds(start, tk)], p)
        acc_sc[hh] = alpha[0:1, :] * acc_sc[hh] + pv
        m_sc[hh] = m_next

    def stage(cur, slot, masked_next):
        for hh in range(2):
            scores(cur + 1, 1 - slot, hh, masked_next)
            update(cur, slot, hh)

    def finish(cur, slot):
        for hh in range(2):
            update(cur, slot, hh)

    @pl.when(qi == 0)
    def _():
        for hh in range(2):
            scores(0, 0, hh, True)
        finish(0, 0)

    @pl.when(qi > 0)
    def _():
        for hh in range(2):
            scores(0, 0, hh, False)

    def body(p, carry):
        stage(2 * p, 0, False)
        stage(2 * p + 1, 1, False)
        return carry

    lax.fori_loop(0, (qi - 1) // 2, body, 0)

    @pl.when(qi % 2 == 1)
    def _():
        stage(qi - 1, 0, True)
        finish(qi, 1)

    @pl.when(jnp.logical_and(qi > 0, qi % 2 == 0))
    def _():
        stage(qi - 2, 0, False)
        stage(qi - 1, 1, True)
        finish(qi, 0)

    outs = []
    for hh in range(2):
        acc = acc_sc[hh]
        outs.append((acc / acc[MLA_V:MLA_V + 1, :]).T)
    lane = lax.broadcasted_iota(jnp.int32, (tq, LANES), 1)
    o_ref[...] = jnp.where(lane < MLA_V, outs[0], pltpu.roll(outs[1], MLA_V, 1)).astype(o_ref.dtype)


def _attention(qt, k, vt, bsz, seq):
    tq = ATTN_TILE
    nq = seq // tq
    pairs = MLA_HEADS // 2
    return pl.pallas_call(
        _attn_kernel,
        grid=(bsz, pairs, nq),
        in_specs=[
            pl.BlockSpec((2 * HEAD_PAD, tq), lambda bi, p, qi: (p, bi * nq + qi)),
            pl.BlockSpec((None, seq, 2 * HEAD_PAD), lambda bi, p, qi: (bi, 0, p)),
            pl.BlockSpec((2 * HEAD_PAD, seq), lambda bi, p, qi: (p, bi)),
        ],
        out_specs=pl.BlockSpec((None, tq, 2 * MLA_V), lambda bi, p, qi: (bi, qi, p)),
        out_shape=jax.ShapeDtypeStruct((bsz, seq, MLA_HEADS * MLA_V), jnp.bfloat16),
        scratch_shapes=[
            pltpu.VMEM((2, 2, ATTN_TILE, tq), jnp.float32),
            pltpu.VMEM((2, 2, SUBLANES, tq), jnp.float32),
            pltpu.VMEM((2, SUBLANES, tq), jnp.float32),
            pltpu.VMEM((2, HEAD_PAD, tq), jnp.float32),
        ],
        compiler_params=_params(("parallel", "parallel", "arbitrary")),
        name="mla_attention",
    )(qt, k, vt)


def _post_kernel(n_mix, *refs):
    h_ref = refs[0]
    mix_refs = refs[1:1 + n_mix]
    wout_refs = refs[1 + n_mix:1 + 2 * n_mix]
    g1_ref, b1_ref, w1_ref, w2_ref, g2_ref, b2_ref, o_ref = refs[1 + 2 * n_mix:]
    mix = _dot(mix_refs[0][...], wout_refs[0][...])
    for m_ref, w_ref in zip(mix_refs[1:], wout_refs[1:]):
        mix = mix + _dot(m_ref[...], w_ref[...])
    y = _layer_norm(DN_ALPHA * h_ref[...] + mix, g1_ref[...], b1_ref[...])
    yb = y.astype(jnp.bfloat16)
    ff = None
    for c in range(D_FF // FF_CHUNK):
        cs = slice(c * FF_CHUNK, (c + 1) * FF_CHUNK)
        a = jnp.maximum(_dot(yb, w1_ref[:, cs]), 0.0)
        part = _dot((a * a).astype(jnp.bfloat16), w2_ref[cs, :])
        ff = part if ff is None else ff + part
    o_ref[...] = _layer_norm(DN_ALPHA * y + ff, g2_ref[...], b2_ref[...])


def _post(h2, mixes, wouts, g1, b1, w1, w2, g2, b2):
    t = h2.shape[0]
    tm = TOKEN_TILE
    row = lambda i: (i, 0)
    n = len(mixes)
    in_specs = [pl.BlockSpec((tm, D_MODEL), row)]
    in_specs += [pl.BlockSpec((tm, m.shape[1]), row) for m in mixes]
    in_specs += [_resident(w.shape) for w in wouts]
    in_specs += [_resident(a.shape) for a in (g1, b1, w1, w2, g2, b2)]
    return pl.pallas_call(
        functools.partial(_post_kernel, n),
        grid=(t // tm,),
        in_specs=in_specs,
        out_specs=pl.BlockSpec((tm, D_MODEL), row),
        out_shape=jax.ShapeDtypeStruct((t, D_MODEL), jnp.float32),
        compiler_params=_params(("parallel",)),
        name="post",
    )(h2, *mixes, *wouts, g1, b1, w1, w2, g2, b2)


def _matmul_kernel(x_ref, w_ref, o_ref):
    xb = x_ref[...].astype(jnp.bfloat16)
    n = w_ref.shape[1]
    for c in range(n // FF_CHUNK):
        cs = slice(c * FF_CHUNK, (c + 1) * FF_CHUNK)
        o_ref[:, cs] = _dot(xb, w_ref[:, cs]).astype(o_ref.dtype)


def _matmul(x2, w):
    t = x2.shape[0]
    tm = TOKEN_TILE
    n = w.shape[1]
    return pl.pallas_call(
        _matmul_kernel,
        grid=(t // tm,),
        in_specs=[pl.BlockSpec((tm, x2.shape[1]), lambda i: (i, 0)), _resident(w.shape)],
        out_specs=pl.BlockSpec((tm, n), lambda i: (i, 0)),
        out_shape=jax.ShapeDtypeStruct((t, n), jnp.bfloat16),
        compiler_params=_params(("parallel",)),
        name="odd_in_proj",
    )(x2, w)


def _midpoint_rows(b, half):
    c = b.shape[0]
    if half >= SUBLANES:
        parts = []
        for p in range(c // (2 * half)):
            r = p * 2 * half + half - 1
            parts.append(jnp.broadcast_to(b[r:r + 1, :], (2 * half, LANES)))
        return parts[0] if len(parts) == 1 else jnp.concatenate(parts, axis=0)
    b3 = b.reshape(c // SUBLANES, SUBLANES, LANES)
    sub = lax.broadcasted_iota(jnp.int32, b3.shape, 1)
    out = None
    for p in range(SUBLANES // (2 * half)):
        r = p * 2 * half + half - 1
        cand = jnp.broadcast_to(b3[:, r:r + 1, :], b3.shape)
        out = cand if out is None else jnp.where(sub >= p * 2 * half, cand, out)
    return out.reshape(c, LANES)


def _hgrn_kernel(z_ref, lb_ref, gn_ref, tril_ref, o_ref, st_ref):
    c = HG_CHUNK
    n_levels = int(math.log2(c))

    @pl.when(pl.program_id(2) == 0)
    def _():
        st_ref[...] = jnp.zeros(st_ref.shape, jnp.float32)

    lbp = lb_ref[...]
    mx = jnp.max(lbp, axis=0, keepdims=True)
    e = jnp.exp(lbp - mx)
    sm = e / jnp.sum(e, axis=0, keepdims=True)
    lb = (sm[0:1, :] + sm[1:2, :]) - sm[0:1, :]
    gn = gn_ref[...]
    tril = tril_ref[...]

    t_idx = lax.broadcasted_iota(jnp.int32, (c, c), 0)
    s_idx = lax.broadcasted_iota(jnp.int32, (c, c), 1)
    x = jnp.bitwise_xor(t_idx, s_idx)
    level = jnp.zeros((c, c), jnp.int32)
    for bit in range(n_levels):
        level = level + (x >= (1 << bit)).astype(jnp.int32)
    level = jnp.where(s_idx <= t_idx, level, -1)
    row_id = lax.broadcasted_iota(jnp.int32, (c, LANES), 0)

    for ci in range(z_ref.shape[0] // c):
        rs = slice(ci * c, (ci + 1) * c)
        qf = z_ref[rs, 0 * HG_DK:1 * HG_DK].astype(jnp.float32)
        ff = z_ref[rs, 1 * HG_DK:2 * HG_DK].astype(jnp.float32)
        vb = z_ref[rs, 2 * HG_DK:3 * HG_DK]
        gf = z_ref[rs, 3 * HG_DK:4 * HG_DK].astype(jnp.float32)
        q = qf * _sigmoid(qf)
        gate = lb + (1.0 - lb) * _sigmoid(ff)
        k = 1.0 - gate
        lg = jnp.log(gate)
        b = jnp.dot(tril, lg, precision=lax.Precision.HIGHEST, preferred_element_type=jnp.float32)
        b_last = b[c - 1:c, :]
        st = st_ref[...]

        o = _dot_nt((q * jnp.exp(b)).astype(jnp.bfloat16), st.astype(jnp.bfloat16))
        qb = q.astype(jnp.bfloat16)
        kb = k.astype(jnp.bfloat16)
        attn = jnp.where(level == 0, _dot_nt(qb, kb), 0.0)
        for lv in range(1, n_levels + 1):
            half = 1 << (lv - 1)
            right = jnp.bitwise_and(row_id, half) != 0
            d = b - _midpoint_rows(b, half)
            decay = jnp.exp(jnp.where(right, d, -d))
            zl = (jnp.where(right, q, k) * decay).astype(jnp.bfloat16)
            attn = jnp.where(level == lv, _dot_nt(zl, zl), attn)
        o = o + _dot(attn.astype(jnp.bfloat16), vb)

        kd = (k * jnp.exp(b_last - b)).astype(jnp.bfloat16)
        upd = lax.dot_general(vb, kd, (((0,), (0,)), ((), ())), preferred_element_type=jnp.float32)
        st_ref[...] = st * jnp.exp(b_last) + upd

        o = o * lax.rsqrt(jnp.mean(o * o, axis=-1, keepdims=True) + NORM_EPS)
        o_ref[rs, :] = (o * gn * (gf * _sigmoid(gf))).astype(o_ref.dtype)


def _hgrn(z, lb, gn, tril):
    b, s, _ = z.shape
    blk = HG_BLOCK
    return pl.pallas_call(
        _hgrn_kernel,
        grid=(b, HG_HEADS, s // blk),
        in_specs=[
            pl.BlockSpec((None, blk, 4 * HG_DK), lambda bi, h, li: (bi, li, h)),
            pl.BlockSpec((DEPTH, HG_DK), lambda bi, h, li: (0, h)),
            pl.BlockSpec((1, HG_DV), lambda bi, h, li: (0, h)),
            _resident(tril.shape),
        ],
        out_specs=pl.BlockSpec((None, blk, HG_DV), lambda bi, h, li: (bi, li, h)),
        out_shape=jax.ShapeDtypeStruct((b, s, HG_HEADS * HG_DV), jnp.bfloat16),
        scratch_shapes=[pltpu.VMEM((HG_DV, HG_DK), jnp.float32)],
        compiler_params=_params(("parallel", "parallel", "arbitrary")),
        name="hgrn2",
    )(z, lb, gn, tril)


def kernel(x, positions, w_in_e, mla_gq, mla_gkv, w_qb, w_kvb, sgu_ln_g, sgu_ln_b, sgu_w, sgu_b, w_out_e,
           w_in_o, hg_lb, hg_gnorm, w_out_o, ln1_g, ln1_b, w_ff1, w_ff2, ln2_g, ln2_b):
    bsz, seq, d = x.shape
    assert d == D_MODEL and hg_lb.shape[0] == DEPTH == 2
    assert seq % ATTN_TILE == 0 and seq % HG_BLOCK == 0 and (bsz * seq) % TOKEN_TILE == 0
    t = bsz * seq
    bf = jnp.bfloat16
    f32 = jnp.float32
    row = lambda a: a.reshape(1, -1).astype(f32)

    inv_freq = ROPE_BASE ** (-jnp.arange(ROPE_HALF, dtype=f32) / ROPE_HALF)
    freq_col = jnp.broadcast_to(inv_freq[:, None], (ROPE_HALF, LANES))

    we = w_in_e[0]
    s0, s1, s2, s3 = MLA_LORA, 2 * MLA_LORA, 2 * MLA_LORA + MLA_ROPE, 2 * MLA_LORA + MLA_ROPE + SGU_DIM
    kr_cols = jnp.pad(we[:, s1:s2], ((0, 0), (ROPE_LO, HEAD_PAD - ROPE_LO - MLA_ROPE)))
    w_in_p = jnp.concatenate([we[:, :s1], kr_cols, we[:, s2:s3], we[:, s3:]], axis=1).astype(bf)
    assert w_in_p.shape[1] == EVEN_COLS
    dq = MLA_NOPE + MLA_ROPE
    wq_p = jnp.pad(w_qb[0].reshape(MLA_LORA, MLA_HEADS, dq), ((0, 0), (0, 0), (0, HEAD_PAD - dq)))
    wqt_p = wq_p.reshape(MLA_LORA, MLA_HEADS * HEAD_PAD).T.astype(bf)
    wkv = w_kvb[0].reshape(MLA_LORA, MLA_HEADS, MLA_NOPE + MLA_V)
    wk_p = jnp.pad(wkv[:, :, :MLA_NOPE], ((0, 0), (0, 0), (0, HEAD_PAD - MLA_NOPE)))
    wk_p = wk_p.reshape(MLA_LORA, MLA_HEADS * HEAD_PAD).astype(bf)
    wv_p = jnp.pad(wkv[:, :, MLA_NOPE:], ((0, 0), (0, 0), (0, HEAD_PAD - MLA_V)))
    wvt_p = wv_p.reshape(MLA_LORA, MLA_HEADS * HEAD_PAD).T.astype(bf)
    one_col = jnp.tile(jnp.zeros((HEAD_PAD,), f32).at[MLA_V].set(1.0), MLA_HEADS)
    one_col = jnp.broadcast_to(one_col[:, None], (MLA_HEADS * HEAD_PAD, LANES))
    sgu_bias = jnp.repeat(sgu_b[0].T, SGU_GROUP_DIM, axis=1).astype(f32)

    x2 = x.reshape(t, D_MODEL)
    qt, k, vt, b_out = _even_front(x2, positions.reshape(1, t), freq_col, w_in_p, row(mla_gq[0]), row(mla_gkv[0]),
                                   wqt_p, wk_p, wvt_p, one_col, row(sgu_ln_g[0]), row(sgu_ln_b[0]),
                                   sgu_w[0].astype(f32), sgu_bias)
    a_out = _attention(qt, k.reshape(bsz, seq, -1), vt, bsz, seq)
    a_rows = MLA_HEADS * MLA_V
    h1 = _post(x2, [a_out.reshape(t, -1), b_out], [w_out_e[0][:a_rows].astype(bf), w_out_e[0][a_rows:].astype(bf)],
               row(ln1_g[0]), row(ln1_b[0]), w_ff1[0].astype(bf), w_ff2[0].astype(bf), row(ln2_g[0]), row(ln2_b[0]))

    wo = w_in_o[0].reshape(D_MODEL, 4, HG_HEADS, HG_DK).transpose(0, 2, 1, 3).reshape(D_MODEL, 4 * HG_HEADS * HG_DK)
    z = _matmul(h1, wo.astype(bf))
    tril = jnp.tril(jnp.ones((HG_CHUNK, HG_CHUNK), f32))
    o = _hgrn(z.reshape(bsz, seq, -1), hg_lb.astype(f32), row(hg_gnorm[0]), tril)
    h2 = _post(h1, [o.reshape(t, -1)], [w_out_o[0].astype(bf)], row(ln1_g[1]), row(ln1_b[1]), w_ff1[1].astype(bf),
               w_ff2[1].astype(bf), row(ln2_g[1]), row(ln2_b[1]))
    return h2.reshape(bsz, seq, D_MODEL)
```

```python
import functools
import math

import jax
import jax.numpy as jnp
from jax import lax
from jax.experimental import pallas as pl
from jax.experimental.pallas import tpu as pltpu

D_MODEL = 1024
DEPTH = 2
MLA_HEADS = 8
MLA_LORA = 256
MLA_NOPE = 64
MLA_ROPE = 32
MLA_V = 64
MLA_SCALE = (MLA_NOPE + MLA_ROPE) ** -0.5
LOG2_E = math.log2(math.e)
ROPE_BASE = 10000.0
SGU_GROUPS = 4
SGU_GROUP_DIM = 128
SGU_DIM = SGU_GROUPS * SGU_GROUP_DIM
SGU_CHUNK = 128
HG_HEADS = 8
HG_DK = 128
HG_DV = 128
D_FF = 4 * D_MODEL
DN_ALPHA = (2 * DEPTH) ** 0.25
NORM_EPS = 1e-5

LANES = 128
SUBLANES = 8
VMEM_LIMIT_BYTES = 56 * 1024 * 1024

TOKEN_TILE = 512
ATTN_TILE = 512
HG_CHUNK = 128
HG_BLOCK = 512
FF_CHUNK = 1024
MASK_VALUE = -1e30
HG_SAFE_EXPONENT = 60.0

HEAD_PAD = LANES
COL_CQ = 0
COL_CKV = COL_CQ + MLA_LORA
COL_KR = COL_CKV + MLA_LORA
COL_U = COL_KR + HEAD_PAD
COL_V = COL_U + SGU_DIM
EVEN_COLS = COL_V + SGU_DIM
ROPE_HALF = MLA_ROPE // 2
ROPE_LO = MLA_NOPE


def _params(semantics):
    return pltpu.CompilerParams(dimension_semantics=semantics, vmem_limit_bytes=VMEM_LIMIT_BYTES)


def _resident(shape):
    zeros = (0,) * len(shape)
    return pl.BlockSpec(shape, lambda *_: zeros, pipeline_mode=pl.Buffered(1))


def _dot(a, b):
    return jnp.dot(a, b, preferred_element_type=jnp.float32)


def _dot_nt(a, b):
    return lax.dot_general(a, b, (((1,), (1,)), ((), ())), preferred_element_type=jnp.float32)


def _sigmoid(x):
    return 1.0 / (1.0 + jnp.exp(-x))


def _gelu_tanh(x):
    c = math.sqrt(2.0 / math.pi)
    return 0.5 * x * (1.0 + jnp.tanh(c * (x + 0.044715 * (x * x * x))))


def _layer_norm(x, g, b):
    mu = jnp.mean(x, axis=-1, keepdims=True)
    xc = x - mu
    var = jnp.mean(xc * xc, axis=-1, keepdims=True)
    return xc * lax.rsqrt(var + NORM_EPS) * g + b


def _rms_norm(x, g):
    return x * lax.rsqrt(jnp.mean(x * x, axis=-1, keepdims=True) + NORM_EPS) * g


def _even_front_kernel(x_ref, pos_ref, freq_ref, w_in_ref, gq_ref, gkv_ref, wqt_ref, wk_ref, wvt_ref, one_ref,
                       lng_ref, lnb_ref, sw_ref, sb_ref, qt_ref, k_ref, vt_ref, b_ref):
    tm = x_ref.shape[0]
    rep = tm // LANES
    xb = x_ref[...].astype(jnp.bfloat16)
    z = _dot(xb, w_in_ref[...])

    ang = jnp.tile(freq_ref[...], (1, rep)) * pos_ref[...].astype(jnp.float32)
    cos_t = jnp.cos(ang)
    sin_t = jnp.sin(ang)

    cq = _rms_norm(z[:, COL_CQ:COL_CQ + MLA_LORA], gq_ref[...] * (MLA_SCALE * LOG2_E)).astype(jnp.bfloat16)
    ckv = _rms_norm(z[:, COL_CKV:COL_CKV + MLA_LORA], gkv_ref[...]).astype(jnp.bfloat16)

    qt = _dot_nt(wqt_ref[...], cq)
    for h in range(MLA_HEADS):
        base = h * HEAD_PAD
        x1 = qt[base + ROPE_LO:base + ROPE_LO + ROPE_HALF]
        x2 = qt[base + ROPE_LO + ROPE_HALF:base + ROPE_LO + MLA_ROPE]
        slab = jnp.concatenate([qt[base:base + ROPE_LO], x1 * cos_t - x2 * sin_t, x2 * cos_t + x1 * sin_t,
                                qt[base + ROPE_LO + MLA_ROPE:base + HEAD_PAD]], axis=0)
        qt_ref[base:base + HEAD_PAD, :] = slab.astype(qt_ref.dtype)

    vt = _dot_nt(wvt_ref[...], ckv) + jnp.tile(one_ref[...], (1, rep))
    vt_ref[...] = vt.astype(vt_ref.dtype)

    ones = jnp.ones((ROPE_LO, tm), jnp.float32)
    zeros = jnp.zeros((HEAD_PAD - ROPE_LO - MLA_ROPE, tm), jnp.float32)
    cos_k = jnp.concatenate([ones, cos_t, cos_t, zeros], axis=0).T
    sin_k = jnp.concatenate([0.0 * ones, -sin_t, sin_t, zeros], axis=0).T
    lane = lax.broadcasted_iota(jnp.int32, (tm, LANES), 1)
    kr = z[:, COL_KR:COL_KR + HEAD_PAD]
    partner = jnp.where(lane < ROPE_LO + ROPE_HALF, pltpu.roll(kr, LANES - ROPE_HALF, 1), pltpu.roll(kr, ROPE_HALF, 1))
    kr = kr * cos_k + partner * sin_k
    kn = _dot(ckv, wk_ref[...])
    for h in range(MLA_HEADS):
        sl = slice(h * HEAD_PAD, (h + 1) * HEAD_PAD)
        k_ref[:, sl] = (kn[:, sl] + kr).astype(k_ref.dtype)

    u = _gelu_tanh(z[:, COL_U:COL_U + SGU_DIM])
    vn = _layer_norm(_gelu_tanh(z[:, COL_V:COL_V + SGU_DIM]), lng_ref[...], lnb_ref[...]).astype(jnp.bfloat16)
    row = lax.broadcasted_iota(jnp.int32, (SGU_CHUNK, SGU_CHUNK), 0)
    col = lax.broadcasted_iota(jnp.int32, (SGU_CHUNK, SGU_CHUNK), 1)
    causal = col <= row
    bias = sb_ref[...]
    for g in range(SGU_GROUPS):
        wg = jnp.where(causal, sw_ref[g], 0.0).astype(jnp.bfloat16)
        gs = slice(g * SGU_GROUP_DIM, (g + 1) * SGU_GROUP_DIM)
        for c in range(tm // SGU_CHUNK):
            rs = slice(c * SGU_CHUNK, (c + 1) * SGU_CHUNK)
            mixed = _dot(wg, vn[rs, gs]) + bias[:, gs]
            b_ref[rs, gs] = (u[rs, gs] * mixed).astype(b_ref.dtype)


def _even_front(x2, pos_row, freq_col, w_in, gq, gkv, wqt, wk, wvt, one_col, lng, lnb, sw, sb):
    t = x2.shape[0]
    tm = TOKEN_TILE
    row = lambda i: (i, 0)
    col = lambda i: (0, i)
    wide = MLA_HEADS * HEAD_PAD
    out_shapes = (
        jax.ShapeDtypeStruct((wide, t), jnp.bfloat16),
        jax.ShapeDtypeStruct((t, wide), jnp.bfloat16),
        jax.ShapeDtypeStruct((wide, t), jnp.bfloat16),
        jax.ShapeDtypeStruct((t, SGU_DIM), jnp.bfloat16),
    )
    consts = (freq_col, w_in, gq, gkv, wqt, wk, wvt, one_col, lng, lnb, sw, sb)
    return pl.pallas_call(
        _even_front_kernel,
        grid=(t // tm,),
        in_specs=[pl.BlockSpec((tm, D_MODEL), row), pl.BlockSpec((1, tm), col)] + [_resident(a.shape) for a in consts],
        out_specs=(
            pl.BlockSpec((wide, tm), col),
            pl.BlockSpec((tm, wide), row),
            pl.BlockSpec((wide, tm), col),
            pl.BlockSpec((tm, SGU_DIM), row),
        ),
        out_shape=out_shapes,
        compiler_params=_params(("parallel",)),
        name="even_front",
    )(x2, pos_row, *consts)


def _attn_kernel(qt_ref, k_ref, vt_ref, o_ref, s_sc, cm_sc, m_sc, acc_sc):
    tq = qt_ref.shape[1]
    tk = ATTN_TILE
    qi = pl.program_id(2)
    m_sc[...] = jnp.full(m_sc.shape, MASK_VALUE, jnp.float32)
    acc_sc[...] = jnp.zeros(acc_sc.shape, jnp.float32)

    def scores(tile, slot, hh, masked):
        start = pl.multiple_of(tile * tk, tk)
        hs = slice(hh * HEAD_PAD, (hh + 1) * HEAD_PAD)
        s = _dot(k_ref[pl.ds(start, tk), hs], qt_ref[hs, :])
        if masked:
            key = lax.broadcasted_iota(jnp.int32, (tk, tq), 0)
            qry = lax.broadcasted_iota(jnp.int32, (tk, tq), 1)
            s = jnp.where(key <= qry, s, MASK_VALUE)
        s_sc[slot, hh] = s
        cm_sc[slot, hh] = jnp.broadcast_to(jnp.max(s, axis=0, keepdims=True), (SUBLANES, tq))

    def update(tile, slot, hh):
        start = pl.multiple_of(tile * tk, tk)
        hs = slice(hh * HEAD_PAD, (hh + 1) * HEAD_PAD)
        m_prev = m_sc[hh]
        m_next = jnp.maximum(m_prev, cm_sc[slot, hh])
        p = jnp.exp2(s_sc[slot, hh] - m_next[0:1, :]).astype(jnp.bfloat16)
        alpha = jnp.exp2(m_prev - m_next)
        pv = _dot(vt_ref[hs, pl.ds(start, tk)], p)
        acc_sc[hh] = alpha[0:1, :] * acc_sc[hh] + pv
        m_sc[hh] = m_next

    def stage(cur, slot, masked_next):
        for hh in range(2):
            scores(cur + 1, 1 - slot, hh, masked_next)
            update(cur, slot, hh)

    def finish(cur, slot):
        for hh in range(2):
            update(cur, slot, hh)

    @pl.when(qi == 0)
    def _():
        for hh in range(2):
            scores(0, 0, hh, True)
        finish(0, 0)

    @pl.when(qi > 0)
    def _():
        for hh in range(2):
            scores(0, 0, hh, False)

    def body(p, carry):
        stage(2 * p, 0, False)
        stage(2 * p + 1, 1, False)
        return carry

    lax.fori_loop(0, (qi - 1) // 2, body, 0)

    @pl.when(qi % 2 == 1)
    def _():
        stage(qi - 1, 0, True)
        finish(qi, 1)

    @pl.when(jnp.logical_and(qi > 0, qi % 2 == 0))
    def _():
        stage(qi - 2, 0, False)
        stage(qi - 1, 1, True)
        finish(qi, 0)

    outs = []
    for hh in range(2):
        acc = acc_sc[hh]
        outs.append((acc / acc[MLA_V:MLA_V + 1, :]).T)
    lane = lax.broadcasted_iota(jnp.int32, (tq, LANES), 1)
    o_ref[...] = jnp.where(lane < MLA_V, outs[0], pltpu.roll(outs[1], MLA_V, 1)).astype(o_ref.dtype)


def _attention(qt, k, vt, bsz, seq):
    tq = ATTN_TILE
    nq = seq // tq
    pairs = MLA_HEADS // 2
    return pl.pallas_call(
        _attn_kernel,
        grid=(bsz, pairs, nq),
        in_specs=[
            pl.BlockSpec((2 * HEAD_PAD, tq), lambda bi, p, qi: (p, bi * nq + qi)),
            pl.BlockSpec((None, seq, 2 * HEAD_PAD), lambda bi, p, qi: (bi, 0, p)),
            pl.BlockSpec((2 * HEAD_PAD, seq), lambda bi, p, qi: (p, bi)),
        ],
        out_specs=pl.BlockSpec((None, tq, 2 * MLA_V), lambda bi, p, qi: (bi, qi, p)),
        out_shape=jax.ShapeDtypeStruct((bsz, seq, MLA_HEADS * MLA_V), jnp.bfloat16),
        scratch_shapes=[
            pltpu.VMEM((2, 2, ATTN_TILE, tq), jnp.float32),
            pltpu.VMEM((2, 2, SUBLANES, tq), jnp.float32),
            pltpu.VMEM((2, SUBLANES, tq), jnp.float32),
            pltpu.VMEM((2, HEAD_PAD, tq), jnp.float32),
        ],
        compiler_params=_params(("parallel", "parallel", "arbitrary")),
        name="mla_attention",
    )(qt, k, vt)


def _post_kernel(n_mix, *refs):
    h_ref = refs[0]
    mix_refs = refs[1:1 + n_mix]
    wout_refs = refs[1 + n_mix:1 + 2 * n_mix]
    g1_ref, b1_ref, w1_ref, w2_ref, g2_ref, b2_ref, o_ref = refs[1 + 2 * n_mix:]
    mix = _dot(mix_refs[0][...], wout_refs[0][...])
    for m_ref, w_ref in zip(mix_refs[1:], wout_refs[1:]):
        mix = mix + _dot(m_ref[...], w_ref[...])
    y = _layer_norm(DN_ALPHA * h_ref[...] + mix, g1_ref[...], b1_ref[...])
    yb = y.astype(jnp.bfloat16)
    ff = None
    for c in range(D_FF // FF_CHUNK):
        cs = slice(c * FF_CHUNK, (c + 1) * FF_CHUNK)
        a = jnp.maximum(_dot(yb, w1_ref[:, cs]), 0.0)
        part = _dot((a * a).astype(jnp.bfloat16), w2_ref[cs, :])
        ff = part if ff is None else ff + part
    o_ref[...] = _layer_norm(DN_ALPHA * y + ff, g2_ref[...], b2_ref[...])


def _post(h2, mixes, wouts, g1, b1, w1, w2, g2, b2):
    t = h2.shape[0]
    tm = TOKEN_TILE
    row = lambda i: (i, 0)
    n = len(mixes)
    in_specs = [pl.BlockSpec((tm, D_MODEL), row)]
    in_specs += [pl.BlockSpec((tm, m.shape[1]), row) for m in mixes]
    in_specs += [_resident(w.shape) for w in wouts]
    in_specs += [_resident(a.shape) for a in (g1, b1, w1, w2, g2, b2)]
    return pl.pallas_call(
        functools.partial(_post_kernel, n),
        grid=(t // tm,),
        in_specs=in_specs,
        out_specs=pl.BlockSpec((tm, D_MODEL), row),
        out_shape=jax.ShapeDtypeStruct((t, D_MODEL), jnp.float32),
        compiler_params=_params(("parallel",)),
        name="post",
    )(h2, *mixes, *wouts, g1, b1, w1, w2, g2, b2)


def _matmul_kernel(x_ref, w_ref, o_ref):
    xb = x_ref[...].astype(jnp.bfloat16)
    n = w_ref.shape[1]
    for c in range(n // FF_CHUNK):
        cs = slice(c * FF_CHUNK, (c + 1) * FF_CHUNK)
        o_ref[:, cs] = _dot(xb, w_ref[:, cs]).astype(o_ref.dtype)


def _matmul(x2, w):
    t = x2.shape[0]
    tm = TOKEN_TILE
    n = w.shape[1]
    return pl.pallas_call(
        _matmul_kernel,
        grid=(t // tm,),
        in_specs=[pl.BlockSpec((tm, x2.shape[1]), lambda i: (i, 0)), _resident(w.shape)],
        out_specs=pl.BlockSpec((tm, n), lambda i: (i, 0)),
        out_shape=jax.ShapeDtypeStruct((t, n), jnp.bfloat16),
        compiler_params=_params(("parallel",)),
        name="odd_in_proj",
    )(x2, w)


def _midpoint_rows(b, half):
    c = b.shape[0]
    if half >= SUBLANES:
        parts = []
        for p in range(c // (2 * half)):
            r = p * 2 * half + half - 1
            parts.append(jnp.broadcast_to(b[r:r + 1, :], (2 * half, LANES)))
        return parts[0] if len(parts) == 1 else jnp.concatenate(parts, axis=0)
    b3 = b.reshape(c // SUBLANES, SUBLANES, LANES)
    sub = lax.broadcasted_iota(jnp.int32, b3.shape, 1)
    out = None
    for p in range(SUBLANES // (2 * half)):
        r = p * 2 * half + half - 1
        cand = jnp.broadcast_to(b3[:, r:r + 1, :], b3.shape)
        out = cand if out is None else jnp.where(sub >= p * 2 * half, cand, out)
    return out.reshape(c, LANES)


def _hgrn_kernel(z_ref, lb_ref, gn_ref, tril_ref, o_ref, st_ref, q_sc, k_sc, b_sc):
    c = HG_CHUNK
    mid = c // 2 - 1
    n_chunks = z_ref.shape[0] // c
    n_levels = int(math.log2(c))

    @pl.when(pl.program_id(2) == 0)
    def _():
        st_ref[...] = jnp.zeros(st_ref.shape, jnp.float32)

    lbp = lb_ref[...]
    mx = jnp.max(lbp, axis=0, keepdims=True)
    e = jnp.exp(lbp - mx)
    sm = e / jnp.sum(e, axis=0, keepdims=True)
    lb = (sm[0:1, :] + sm[1:2, :]) - sm[0:1, :]
    gn = gn_ref[...]

    qf = z_ref[:, 0 * HG_DK:1 * HG_DK].astype(jnp.float32)
    ff = z_ref[:, 1 * HG_DK:2 * HG_DK].astype(jnp.float32)
    gate = lb + (1.0 - lb) * _sigmoid(ff)
    q_sc[...] = qf * _sigmoid(qf)
    k_sc[...] = 1.0 - gate
    lg = jnp.log(gate)
    hi = lg.astype(jnp.bfloat16)
    r1 = lg - hi.astype(jnp.float32)
    md = r1.astype(jnp.bfloat16)
    lo = (r1 - md.astype(jnp.float32)).astype(jnp.bfloat16)
    parts = jnp.concatenate([hi, md, lo], axis=1)
    tril = tril_ref[...]
    risk = None
    for ci in range(n_chunks):
        rs = slice(ci * c, (ci + 1) * c)
        b3 = _dot(tril, parts[rs])
        b = (b3[:, 0:HG_DK] + b3[:, HG_DK:2 * HG_DK]) + b3[:, 2 * HG_DK:3 * HG_DK]
        b_sc[rs] = b
        r = jnp.maximum(b[0:1, :] - b[mid:mid + 1, :], b[mid:mid + 1, :] - b[c - 1:c, :])
        risk = r if risk is None else jnp.maximum(risk, r)
    safe = jnp.max(risk) <= HG_SAFE_EXPONENT

    t_idx = lax.broadcasted_iota(jnp.int32, (c, c), 0)
    s_idx = lax.broadcasted_iota(jnp.int32, (c, c), 1)

    def finish_chunk(rs, q, k, b, attn):
        vb = z_ref[rs, 2 * HG_DK:3 * HG_DK]
        gf = z_ref[rs, 3 * HG_DK:4 * HG_DK].astype(jnp.float32)
        b_last = b[c - 1:c, :]
        st = st_ref[...]
        o = _dot_nt((q * jnp.exp(b)).astype(jnp.bfloat16), st.astype(jnp.bfloat16))
        o = o + _dot(attn.astype(jnp.bfloat16), vb)
        kd = (k * jnp.exp(b_last - b)).astype(jnp.bfloat16)
        upd = lax.dot_general(vb, kd, (((0,), (0,)), ((), ())), preferred_element_type=jnp.float32)
        st_ref[...] = st * jnp.exp(b_last) + upd
        o = o * lax.rsqrt(jnp.mean(o * o, axis=-1, keepdims=True) + NORM_EPS)
        o_ref[rs, :] = (o * gn * (gf * _sigmoid(gf))).astype(o_ref.dtype)

    @pl.when(safe)
    def _():
        causal = s_idx <= t_idx
        for ci in range(n_chunks):
            rs = slice(ci * c, (ci + 1) * c)
            q, k, b = q_sc[rs], k_sc[rs], b_sc[rs]
            d = b - b[mid:mid + 1, :]
            qe = (q * jnp.exp(d)).astype(jnp.bfloat16)
            ke = (k * jnp.exp(-d)).astype(jnp.bfloat16)
            attn = jnp.where(causal, _dot_nt(qe, ke), 0.0)
            finish_chunk(rs, q, k, b, attn)

    @pl.when(jnp.logical_not(safe))
    def _():
        x = jnp.bitwise_xor(t_idx, s_idx)
        level = jnp.zeros((c, c), jnp.int32)
        for bit in range(n_levels):
            level = level + (x >= (1 << bit)).astype(jnp.int32)
        level = jnp.where(s_idx <= t_idx, level, -1)
        row_id = lax.broadcasted_iota(jnp.int32, (c, LANES), 0)
        for ci in range(n_chunks):
            rs = slice(ci * c, (ci + 1) * c)
            q, k, b = q_sc[rs], k_sc[rs], b_sc[rs]
            attn = jnp.where(level == 0, _dot_nt(q.astype(jnp.bfloat16), k.astype(jnp.bfloat16)), 0.0)
            for lv in range(1, n_levels + 1):
                half = 1 << (lv - 1)
                right = jnp.bitwise_and(row_id, half) != 0
                d = b - _midpoint_rows(b, half)
                decay = jnp.exp(jnp.where(right, d, -d))
                zl = (jnp.where(right, q, k) * decay).astype(jnp.bfloat16)
                attn = jnp.where(level == lv, _dot_nt(zl, zl), attn)
            finish_chunk(rs, q, k, b, attn)


def _hgrn(z, lb, gn, tril):
    b, s, _ = z.shape
    blk = HG_BLOCK
    return pl.pallas_call(
        _hgrn_kernel,
        grid=(b, HG_HEADS, s // blk),
        in_specs=[
            pl.BlockSpec((None, blk, 4 * HG_DK), lambda bi, h, li: (bi, li, h)),
            pl.BlockSpec((DEPTH, HG_DK), lambda bi, h, li: (0, h)),
            pl.BlockSpec((1, HG_DV), lambda bi, h, li: (0, h)),
            _resident(tril.shape),
        ],
        out_specs=pl.BlockSpec((None, blk, HG_DV), lambda bi, h, li: (bi, li, h)),
        out_shape=jax.ShapeDtypeStruct((b, s, HG_HEADS * HG_DV), jnp.bfloat16),
        scratch_shapes=[pltpu.VMEM((HG_DV, HG_DK), jnp.float32)] + [pltpu.VMEM((blk, HG_DK), jnp.float32)] * 3,
        compiler_params=_params(("parallel", "parallel", "arbitrary")),
        name="hgrn2",
    )(z, lb, gn, tril)


def kernel(x, positions, w_in_e, mla_gq, mla_gkv, w_qb, w_kvb, sgu_ln_g, sgu_ln_b, sgu_w, sgu_b, w_out_e,
           w_in_o, hg_lb, hg_gnorm, w_out_o, ln1_g, ln1_b, w_ff1, w_ff2, ln2_g, ln2_b):
    bsz, seq, d = x.shape
    assert d == D_MODEL and hg_lb.shape[0] == DEPTH == 2
    assert seq % ATTN_TILE == 0 and seq % HG_BLOCK == 0 and (bsz * seq) % TOKEN_TILE == 0
    t = bsz * seq
    bf = jnp.bfloat16
    f32 = jnp.float32
    row = lambda a: a.reshape(1, -1).astype(f32)

    inv_freq = ROPE_BASE ** (-jnp.arange(ROPE_HALF, dtype=f32) / ROPE_HALF)
    freq_col = jnp.broadcast_to(inv_freq[:, None], (ROPE_HALF, LANES))

    we = w_in_e[0]
    s0, s1, s2, s3 = MLA_LORA, 2 * MLA_LORA, 2 * MLA_LORA + MLA_ROPE, 2 * MLA_LORA + MLA_ROPE + SGU_DIM
    kr_cols = jnp.pad(we[:, s1:s2], ((0, 0), (ROPE_LO, HEAD_PAD - ROPE_LO - MLA_ROPE)))
    w_in_p = jnp.concatenate([we[:, :s1], kr_cols, we[:, s2:s3], we[:, s3:]], axis=1).astype(bf)
    assert w_in_p.shape[1] == EVEN_COLS
    dq = MLA_NOPE + MLA_ROPE
    wq_p = jnp.pad(w_qb[0].reshape(MLA_LORA, MLA_HEADS, dq), ((0, 0), (0, 0), (0, HEAD_PAD - dq)))
    wqt_p = wq_p.reshape(MLA_LORA, MLA_HEADS * HEAD_PAD).T.astype(bf)
    wkv = w_kvb[0].reshape(MLA_LORA, MLA_HEADS, MLA_NOPE + MLA_V)
    wk_p = jnp.pad(wkv[:, :, :MLA_NOPE], ((0, 0), (0, 0), (0, HEAD_PAD - MLA_NOPE)))
    wk_p = wk_p.reshape(MLA_LORA, MLA_HEADS * HEAD_PAD).astype(bf)
    wv_p = jnp.pad(wkv[:, :, MLA_NOPE:], ((0, 0), (0, 0), (0, HEAD_PAD - MLA_V)))
    wvt_p = wv_p.reshape(MLA_LORA, MLA_HEADS * HEAD_PAD).T.astype(bf)
    one_col = jnp.tile(jnp.zeros((HEAD_PAD,), f32).at[MLA_V].set(1.0), MLA_HEADS)
    one_col = jnp.broadcast_to(one_col[:, None], (MLA_HEADS * HEAD_PAD, LANES))
    sgu_bias = jnp.repeat(sgu_b[0].T, SGU_GROUP_DIM, axis=1).astype(f32)

    x2 = x.reshape(t, D_MODEL)
    qt, k, vt, b_out = _even_front(x2, positions.reshape(1, t), freq_col, w_in_p, row(mla_gq[0]), row(mla_gkv[0]),
                                   wqt_p, wk_p, wvt_p, one_col, row(sgu_ln_g[0]), row(sgu_ln_b[0]),
                                   sgu_w[0].astype(f32), sgu_bias)
    a_out = _attention(qt, k.reshape(bsz, seq, -1), vt, bsz, seq)
    a_rows = MLA_HEADS * MLA_V
    h1 = _post(x2, [a_out.reshape(t, -1), b_out], [w_out_e[0][:a_rows].astype(bf), w_out_e[0][a_rows:].astype(bf)],
               row(ln1_g[0]), row(ln1_b[0]), w_ff1[0].astype(bf), w_ff2[0].astype(bf), row(ln2_g[0]), row(ln2_b[0]))

    wo = w_in_o[0].reshape(D_MODEL, 4, HG_HEADS, HG_DK).transpose(0, 2, 1, 3).reshape(D_MODEL, 4 * HG_HEADS * HG_DK)
    z = _matmul(h1, wo.astype(bf))
    tril = jnp.tril(jnp.ones((HG_CHUNK, HG_CHUNK), bf))
    o = _hgrn(z.reshape(bsz, seq, -1), hg_lb.astype(f32), row(hg_gnorm[0]), tril)
    h2 = _post(h1, [o.reshape(t, -1)], [w_out_o[0].astype(bf)], row(ln1_g[1]), row(ln1_b[1]), w_ff1[1].astype(bf),
               w_ff2[1].astype(bf), row(ln2_g[1]), row(ln2_b[1]))
    return h2.reshape(bsz, seq, D_MODEL)
```

```python
import functools
import math

import jax
import jax.numpy as jnp
from jax import lax
from jax.experimental import pallas as pl
from jax.experimental.pallas import tpu as pltpu

D_MODEL = 1024
DEPTH = 2
MLA_HEADS = 8
MLA_LORA = 256
MLA_NOPE = 64
MLA_ROPE = 32
MLA_V = 64
MLA_SCALE = (MLA_NOPE + MLA_ROPE) ** -0.5
LOG2_E = math.log2(math.e)
ROPE_BASE = 10000.0
SGU_GROUPS = 4
SGU_GROUP_DIM = 128
SGU_DIM = SGU_GROUPS * SGU_GROUP_DIM
SGU_CHUNK = 128
HG_HEADS = 8
HG_DK = 128
HG_DV = 128
D_FF = 4 * D_MODEL
DN_ALPHA = (2 * DEPTH) ** 0.25
NORM_EPS = 1e-5

LANES = 128
SUBLANES = 8
VMEM_LIMIT_BYTES = 56 * 1024 * 1024

TOKEN_TILE = 512
ATTN_TILE = 512
HG_CHUNK = 128
HG_BLOCK = 2048
FF_CHUNK = 1024
MASK_VALUE = -1e30
HG_SAFE_EXPONENT = 60.0

HEAD_PAD = LANES
COL_CQ = 0
COL_CKV = COL_CQ + MLA_LORA
COL_KR = COL_CKV + MLA_LORA
COL_U = COL_KR + HEAD_PAD
COL_V = COL_U + SGU_DIM
EVEN_COLS = COL_V + SGU_DIM
ROPE_HALF = MLA_ROPE // 2
ROPE_LO = MLA_NOPE


def _params(semantics):
    return pltpu.CompilerParams(dimension_semantics=semantics, vmem_limit_bytes=VMEM_LIMIT_BYTES)


def _resident(shape):
    zeros = (0,) * len(shape)
    return pl.BlockSpec(shape, lambda *_: zeros, pipeline_mode=pl.Buffered(1))


def _dot(a, b):
    return jnp.dot(a, b, preferred_element_type=jnp.float32)


def _dot_nt(a, b):
    return lax.dot_general(a, b, (((1,), (1,)), ((), ())), preferred_element_type=jnp.float32)


def _sigmoid(x):
    return 0.5 * jnp.tanh(0.5 * x) + 0.5


def _gelu_tanh(x):
    c = math.sqrt(2.0 / math.pi)
    return 0.5 * x * (1.0 + jnp.tanh(c * (x + 0.044715 * (x * x * x))))


def _layer_norm(x, g, b):
    mu = jnp.mean(x, axis=-1, keepdims=True)
    xc = x - mu
    var = jnp.mean(xc * xc, axis=-1, keepdims=True)
    return xc * lax.rsqrt(var + NORM_EPS) * g + b


def _rms_norm(x, g):
    return x * lax.rsqrt(jnp.mean(x * x, axis=-1, keepdims=True) + NORM_EPS) * g


def _even_front_kernel(x_ref, pos_ref, freq_ref, w_in_ref, gq_ref, gkv_ref, wqt_ref, wk_ref, wvt_ref, one_ref,
                       lng_ref, lnb_ref, sw_ref, sb_ref, qt_ref, k_ref, vt_ref, b_ref):
    tm = x_ref.shape[0]
    rep = tm // LANES
    xb = x_ref[...].astype(jnp.bfloat16)
    z = _dot(xb, w_in_ref[...])

    ang = jnp.tile(freq_ref[...], (1, rep)) * pos_ref[...].astype(jnp.float32)
    cos_t = jnp.cos(ang)
    sin_t = jnp.sin(ang)

    cq = _rms_norm(z[:, COL_CQ:COL_CQ + MLA_LORA], gq_ref[...] * (MLA_SCALE * LOG2_E)).astype(jnp.bfloat16)
    ckv = _rms_norm(z[:, COL_CKV:COL_CKV + MLA_LORA], gkv_ref[...]).astype(jnp.bfloat16)

    qt = _dot_nt(wqt_ref[...], cq)
    for h in range(MLA_HEADS):
        base = h * HEAD_PAD
        x1 = qt[base + ROPE_LO:base + ROPE_LO + ROPE_HALF]
        x2 = qt[base + ROPE_LO + ROPE_HALF:base + ROPE_LO + MLA_ROPE]
        slab = jnp.concatenate([qt[base:base + ROPE_LO], x1 * cos_t - x2 * sin_t, x2 * cos_t + x1 * sin_t,
                                qt[base + ROPE_LO + MLA_ROPE:base + HEAD_PAD]], axis=0)
        qt_ref[base:base + HEAD_PAD, :] = slab.astype(qt_ref.dtype)

    vt = _dot_nt(wvt_ref[...], ckv) + jnp.tile(one_ref[...], (1, rep))
    vt_ref[...] = vt.astype(vt_ref.dtype)

    ones = jnp.ones((ROPE_LO, tm), jnp.float32)
    zeros = jnp.zeros((HEAD_PAD - ROPE_LO - MLA_ROPE, tm), jnp.float32)
    cos_k = jnp.concatenate([ones, cos_t, cos_t, zeros], axis=0).T
    sin_k = jnp.concatenate([0.0 * ones, -sin_t, sin_t, zeros], axis=0).T
    lane = lax.broadcasted_iota(jnp.int32, (tm, LANES), 1)
    kr = z[:, COL_KR:COL_KR + HEAD_PAD]
    partner = jnp.where(lane < ROPE_LO + ROPE_HALF, pltpu.roll(kr, LANES - ROPE_HALF, 1), pltpu.roll(kr, ROPE_HALF, 1))
    kr = kr * cos_k + partner * sin_k
    kn = _dot(ckv, wk_ref[...])
    for h in range(MLA_HEADS):
        sl = slice(h * HEAD_PAD, (h + 1) * HEAD_PAD)
        k_ref[:, sl] = (kn[:, sl] + kr).astype(k_ref.dtype)

    u = _gelu_tanh(z[:, COL_U:COL_U + SGU_DIM])
    vn = _layer_norm(_gelu_tanh(z[:, COL_V:COL_V + SGU_DIM]), lng_ref[...], lnb_ref[...]).astype(jnp.bfloat16)
    row = lax.broadcasted_iota(jnp.int32, (SGU_CHUNK, SGU_CHUNK), 0)
    col = lax.broadcasted_iota(jnp.int32, (SGU_CHUNK, SGU_CHUNK), 1)
    causal = col <= row
    bias = sb_ref[...]
    for g in range(SGU_GROUPS):
        wg = jnp.where(causal, sw_ref[g], 0.0).astype(jnp.bfloat16)
        gs = slice(g * SGU_GROUP_DIM, (g + 1) * SGU_GROUP_DIM)
        for c in range(tm // SGU_CHUNK):
            rs = slice(c * SGU_CHUNK, (c + 1) * SGU_CHUNK)
            mixed = _dot(wg, vn[rs, gs]) + bias[:, gs]
            b_ref[rs, gs] = (u[rs, gs] * mixed).astype(b_ref.dtype)


def _even_front(x2, pos_row, freq_col, w_in, gq, gkv, wqt, wk, wvt, one_col, lng, lnb, sw, sb):
    t = x2.shape[0]
    tm = TOKEN_TILE
    row = lambda i: (i, 0)
    col = lambda i: (0, i)
    wide = MLA_HEADS * HEAD_PAD
    out_shapes = (
        jax.ShapeDtypeStruct((wide, t), jnp.bfloat16),
        jax.ShapeDtypeStruct((t, wide), jnp.bfloat16),
        jax.ShapeDtypeStruct((wide, t), jnp.bfloat16),
        jax.ShapeDtypeStruct((t, SGU_DIM), jnp.bfloat16),
    )
    consts = (freq_col, w_in, gq, gkv, wqt, wk, wvt, one_col, lng, lnb, sw, sb)
    return pl.pallas_call(
        _even_front_kernel,
        grid=(t // tm,),
        in_specs=[pl.BlockSpec((tm, D_MODEL), row), pl.BlockSpec((1, tm), col)] + [_resident(a.shape) for a in consts],
        out_specs=(
            pl.BlockSpec((wide, tm), col),
            pl.BlockSpec((tm, wide), row),
            pl.BlockSpec((wide, tm), col),
            pl.BlockSpec((tm, SGU_DIM), row),
        ),
        out_shape=out_shapes,
        compiler_params=_params(("parallel",)),
        name="even_front",
    )(x2, pos_row, *consts)


def _attn_kernel(qt_ref, k_ref, vt_ref, o_ref, s_sc, cm_sc, m_sc, acc_sc):
    tq = qt_ref.shape[1]
    tk = ATTN_TILE
    qi = pl.program_id(2)
    m_sc[...] = jnp.full(m_sc.shape, MASK_VALUE, jnp.float32)
    acc_sc[...] = jnp.zeros(acc_sc.shape, jnp.float32)

    def scores(tile, slot, hh, masked):
        start = pl.multiple_of(tile * tk, tk)
        hs = slice(hh * HEAD_PAD, (hh + 1) * HEAD_PAD)
        s = _dot(k_ref[pl.ds(start, tk), hs], qt_ref[hs, :])
        if masked:
            key = lax.broadcasted_iota(jnp.int32, (tk, tq), 0)
            qry = lax.broadcasted_iota(jnp.int32, (tk, tq), 1)
            s = jnp.where(key <= qry, s, MASK_VALUE)
        s_sc[slot, hh] = s
        cm_sc[slot, hh] = jnp.broadcast_to(jnp.max(s, axis=0, keepdims=True), (SUBLANES, tq))

    def update(tile, slot, hh):
        start = pl.multiple_of(tile * tk, tk)
        hs = slice(hh * HEAD_PAD, (hh + 1) * HEAD_PAD)
        m_prev = m_sc[hh]
        m_next = jnp.maximum(m_prev, cm_sc[slot, hh])
        p = jnp.exp2(s_sc[slot, hh] - m_next[0:1, :]).astype(jnp.bfloat16)
        alpha = jnp.exp2(m_prev - m_next)
        pv = _dot(vt_ref[hs, pl.ds(start, tk)], p)
        acc_sc[hh] = alpha[0:1, :] * acc_sc[hh] + pv
        m_sc[hh] = m_next

    def stage(cur, slot, masked_next):
        for hh in range(2):
            scores(cur + 1, 1 - slot, hh, masked_next)
            update(cur, slot, hh)

    def finish(cur, slot):
        for hh in range(2):
            update(cur, slot, hh)

    @pl.when(qi == 0)
    def _():
        for hh in range(2):
            scores(0, 0, hh, True)
        finish(0, 0)

    @pl.when(qi > 0)
    def _():
        for hh in range(2):
            scores(0, 0, hh, False)

    def body(p, carry):
        stage(2 * p, 0, False)
        stage(2 * p + 1, 1, False)
        return carry

    lax.fori_loop(0, (qi - 1) // 2, body, 0)

    @pl.when(qi % 2 == 1)
    def _():
        stage(qi - 1, 0, True)
        finish(qi, 1)

    @pl.when(jnp.logical_and(qi > 0, qi % 2 == 0))
    def _():
        stage(qi - 2, 0, False)
        stage(qi - 1, 1, True)
        finish(qi, 0)

    outs = []
    for hh in range(2):
        acc = acc_sc[hh]
        outs.append((acc / acc[MLA_V:MLA_V + 1, :]).T)
    lane = lax.broadcasted_iota(jnp.int32, (tq, LANES), 1)
    o_ref[...] = jnp.where(lane < MLA_V, outs[0], pltpu.roll(outs[1], MLA_V, 1)).astype(o_ref.dtype)


def _attention(qt, k, vt, bsz, seq):
    tq = ATTN_TILE
    nq = seq // tq
    pairs = MLA_HEADS // 2
    return pl.pallas_call(
        _attn_kernel,
        grid=(bsz, pairs, nq),
        in_specs=[
            pl.BlockSpec((2 * HEAD_PAD, tq), lambda bi, p, qi: (p, bi * nq + qi)),
            pl.BlockSpec((None, seq, 2 * HEAD_PAD), lambda bi, p, qi: (bi, 0, p)),
            pl.BlockSpec((2 * HEAD_PAD, seq), lambda bi, p, qi: (p, bi)),
        ],
        out_specs=pl.BlockSpec((None, tq, 2 * MLA_V), lambda bi, p, qi: (bi, qi, p)),
        out_shape=jax.ShapeDtypeStruct((bsz, seq, MLA_HEADS * MLA_V), jnp.bfloat16),
        scratch_shapes=[
            pltpu.VMEM((2, 2, ATTN_TILE, tq), jnp.float32),
            pltpu.VMEM((2, 2, SUBLANES, tq), jnp.float32),
            pltpu.VMEM((2, SUBLANES, tq), jnp.float32),
            pltpu.VMEM((2, HEAD_PAD, tq), jnp.float32),
        ],
        compiler_params=_params(("parallel", "parallel", "arbitrary")),
        name="mla_attention",
    )(qt, k, vt)


def _post_kernel(n_mix, *refs):
    h_ref = refs[0]
    mix_refs = refs[1:1 + n_mix]
    wout_refs = refs[1 + n_mix:1 + 2 * n_mix]
    g1_ref, b1_ref, w1_ref, w2_ref, g2_ref, b2_ref, o_ref = refs[1 + 2 * n_mix:]
    tm = h_ref.shape[0]
    halves = [slice(i * (tm // 2), (i + 1) * (tm // 2)) for i in range(2)]
    mixes = []
    for rows in halves:
        mix = _dot(mix_refs[0][rows, :], wout_refs[0][...])
        for m_ref, w_ref in zip(mix_refs[1:], wout_refs[1:]):
            mix = mix + _dot(m_ref[rows, :], w_ref[...])
        mixes.append(mix)
    for rows, mix in zip(halves, mixes):
        y = _layer_norm(DN_ALPHA * h_ref[rows, :] + mix, g1_ref[...], b1_ref[...])
        yb = y.astype(jnp.bfloat16)
        ff = None
        for c in range(D_FF // FF_CHUNK):
            cs = slice(c * FF_CHUNK, (c + 1) * FF_CHUNK)
            a = jnp.maximum(_dot(yb, w1_ref[:, cs]), 0.0)
            part = _dot((a * a).astype(jnp.bfloat16), w2_ref[cs, :])
            ff = part if ff is None else ff + part
        o_ref[rows, :] = _layer_norm(DN_ALPHA * y + ff, g2_ref[...], b2_ref[...])


def _post(h2, mixes, wouts, g1, b1, w1, w2, g2, b2):
    t = h2.shape[0]
    tm = TOKEN_TILE
    row = lambda i: (i, 0)
    n = len(mixes)
    in_specs = [pl.BlockSpec((tm, D_MODEL), row)]
    in_specs += [pl.BlockSpec((tm, m.shape[1]), row) for m in mixes]
    in_specs += [_resident(w.shape) for w in wouts]
    in_specs += [_resident(a.shape) for a in (g1, b1, w1, w2, g2, b2)]
    return pl.pallas_call(
        functools.partial(_post_kernel, n),
        grid=(t // tm,),
        in_specs=in_specs,
        out_specs=pl.BlockSpec((tm, D_MODEL), row),
        out_shape=jax.ShapeDtypeStruct((t, D_MODEL), jnp.float32),
        compiler_params=_params(("parallel",)),
        name="post",
    )(h2, *mixes, *wouts, g1, b1, w1, w2, g2, b2)


HG_SLABS = ("q", "k", "lg_hi", "lg_mid", "lg_lo", "v", "gate")
HG_SLAB = {name: slice(i * HG_DK, (i + 1) * HG_DK) for i, name in enumerate(HG_SLABS)}
HG_WIDE = len(HG_SLABS) * HG_DK


def _odd_front_kernel(x_ref, w_ref, lb_ref, gn_ref, o_ref, risk_ref):
    tm = x_ref.shape[0]
    half = HG_CHUNK // 2
    xb = x_ref[...].astype(jnp.bfloat16)
    lbp = lb_ref[...]
    mx = jnp.max(lbp, axis=0, keepdims=True)
    e = jnp.exp(lbp - mx)
    sm = e / jnp.sum(e, axis=0, keepdims=True)
    lb_all = (sm[0:1, :] + sm[1:2, :]) - sm[0:1, :]
    for h in range(HG_HEADS):
        hs = slice(h * HG_DK, (h + 1) * HG_DK)
        z = _dot(xb, w_ref[:, h * 4 * HG_DK:(h + 1) * 4 * HG_DK])
        zq, zf, zi, zg = (z[:, i * HG_DK:(i + 1) * HG_DK] for i in range(4))
        lb = lb_all[:, hs]
        gate = lb + (1.0 - lb) * _sigmoid(zf)
        lg = jnp.log(gate)
        hi = lg.astype(jnp.bfloat16)
        r1 = lg - hi.astype(jnp.float32)
        md = r1.astype(jnp.bfloat16)
        lo = (r1 - md.astype(jnp.float32)).astype(jnp.bfloat16)
        base = h * HG_WIDE
        slabs = {"q": zq * _sigmoid(zq), "k": 1.0 - gate, "lg_hi": hi, "lg_mid": md, "lg_lo": lo, "v": zi,
                 "gate": gn_ref[:, hs] * (zg * _sigmoid(zg))}
        for name, val in slabs.items():
            sl = HG_SLAB[name]
            o_ref[:, base + sl.start:base + sl.stop] = val.astype(o_ref.dtype)
        risk = None
        for r0 in range(0, tm, half):
            s = -jnp.sum(lg[r0:r0 + half], axis=0, keepdims=True)
            risk = s if risk is None else jnp.maximum(risk, s)
        risk_ref[h:h + 1, :] = jnp.broadcast_to(jnp.max(risk, axis=1, keepdims=True), (1, LANES))


def _odd_front(x2, w, lb, gn):
    t = x2.shape[0]
    tm = TOKEN_TILE
    return pl.pallas_call(
        _odd_front_kernel,
        grid=(t // tm,),
        in_specs=[pl.BlockSpec((tm, x2.shape[1]), lambda i: (i, 0)), _resident(w.shape), _resident(lb.shape),
                  _resident(gn.shape)],
        out_specs=(pl.BlockSpec((tm, HG_HEADS * HG_WIDE), lambda i: (i, 0)),
                   pl.BlockSpec((None, HG_HEADS, LANES), lambda i: (i, 0, 0))),
        out_shape=(jax.ShapeDtypeStruct((t, HG_HEADS * HG_WIDE), jnp.bfloat16),
                   jax.ShapeDtypeStruct((t // tm, HG_HEADS, LANES), jnp.float32)),
        compiler_params=_params(("parallel",)),
        name="odd_front",
    )(x2, w, lb, gn)


def _midpoint_rows(b, half):
    c = b.shape[0]
    if half >= SUBLANES:
        parts = []
        for p in range(c // (2 * half)):
            r = p * 2 * half + half - 1
            parts.append(jnp.broadcast_to(b[r:r + 1, :], (2 * half, LANES)))
        return parts[0] if len(parts) == 1 else jnp.concatenate(parts, axis=0)
    b3 = b.reshape(c // SUBLANES, SUBLANES, LANES)
    sub = lax.broadcasted_iota(jnp.int32, b3.shape, 1)
    out = None
    for p in range(SUBLANES // (2 * half)):
        r = p * 2 * half + half - 1
        cand = jnp.broadcast_to(b3[:, r:r + 1, :], b3.shape)
        out = cand if out is None else jnp.where(sub >= p * 2 * half, cand, out)
    return out.reshape(c, LANES)


def _hgrn_kernel(safe_ref, z_ref, tril_ref, o_ref, st_ref):
    c = HG_CHUNK
    mid = c // 2 - 1
    n_chunks = z_ref.shape[0] // c
    n_levels = int(math.log2(c))
    bi, h, li = pl.program_id(0), pl.program_id(1), pl.program_id(2)
    safe = safe_ref[bi * pl.num_programs(2) + li, h] != 0

    @pl.when(li == 0)
    def _():
        st_ref[...] = jnp.zeros(st_ref.shape, jnp.float32)

    tril = tril_ref[...]

    def slab(rs, name):
        return z_ref[rs, HG_SLAB[name]]

    def cumsum(rs):
        return (_dot(tril, slab(rs, "lg_hi")) + _dot(tril, slab(rs, "lg_mid"))) + _dot(tril, slab(rs, "lg_lo"))

    def load_qk(rs):
        return slab(rs, "q").astype(jnp.float32), slab(rs, "k").astype(jnp.float32)

    t_idx = lax.broadcasted_iota(jnp.int32, (c, c), 0)
    s_idx = lax.broadcasted_iota(jnp.int32, (c, c), 1)

    def state_update(rs, kd, b_last):
        upd = lax.dot_general(slab(rs, "v"), kd, (((0,), (0,)), ((), ())), preferred_element_type=jnp.float32)
        return jnp.exp(b_last), upd

    def chunk_output(rs, qd, st, attn):
        o = _dot_nt(qd, st.astype(jnp.bfloat16))
        o = o + _dot(attn.astype(jnp.bfloat16), slab(rs, "v"))
        o = o * lax.rsqrt(jnp.mean(o * o, axis=-1, keepdims=True) + NORM_EPS)
        o_ref[rs, :] = (o * slab(rs, "gate").astype(jnp.float32)).astype(o_ref.dtype)

    @pl.when(safe)
    def _():
        causal = s_idx <= t_idx
        chunks = [slice(ci * c, (ci + 1) * c) for ci in range(n_chunks)]
        bs = [cumsum(rs) for rs in chunks]
        scores, decays, upds, qds = [], [], [], []
        for rs, b in zip(chunks, bs):
            q, k = load_qk(rs)
            b_mid, b_last = b[mid:mid + 1, :], b[c - 1:c, :]
            d = b - b_mid
            qe = q * jnp.exp(d)
            ke = k * jnp.exp(-d)
            scores.append(_dot_nt(qe.astype(jnp.bfloat16), ke.astype(jnp.bfloat16)))
            decay, upd = state_update(rs, (ke * jnp.exp(b_last - b_mid)).astype(jnp.bfloat16), b_last)
            decays.append(decay)
            upds.append(upd)
            qds.append((qe * jnp.exp(b_mid)).astype(jnp.bfloat16))
        states = [st_ref[...]]
        for ci in range(n_chunks):
            states.append(states[-1] * decays[ci] + upds[ci])
        st_ref[...] = states[-1]
        for ci, rs in enumerate(chunks):
            chunk_output(rs, qds[ci], states[ci], jnp.where(causal, scores[ci], 0.0))

    @pl.when(jnp.logical_not(safe))
    def _():
        x = jnp.bitwise_xor(t_idx, s_idx)
        level = jnp.zeros((c, c), jnp.int32)
        for bit in range(n_levels):
            level = level + (x >= (1 << bit)).astype(jnp.int32)
        level = jnp.where(s_idx <= t_idx, level, -1)
        row_id = lax.broadcasted_iota(jnp.int32, (c, LANES), 0)

        def chunk_body(ci, carry):
            rs = pl.ds(pl.multiple_of(ci * c, c), c)
            (q, k), b = load_qk(rs), cumsum(rs)
            b_last = b[c - 1:c, :]
            attn = jnp.where(level == 0, _dot_nt(q.astype(jnp.bfloat16), k.astype(jnp.bfloat16)), 0.0)
            for lv in range(1, n_levels + 1):
                half = 1 << (lv - 1)
                right = jnp.bitwise_and(row_id, half) != 0
                d = b - _midpoint_rows(b, half)
                decay = jnp.exp(jnp.where(right, d, -d))
                zl = (jnp.where(right, q, k) * decay).astype(jnp.bfloat16)
                attn = jnp.where(level == lv, _dot_nt(zl, zl), attn)
            st = st_ref[...]
            decay, upd = state_update(rs, (k * jnp.exp(b_last - b)).astype(jnp.bfloat16), b_last)
            chunk_output(rs, (q * jnp.exp(b)).astype(jnp.bfloat16), st, attn)
            st_ref[...] = st * decay + upd
            return carry

        lax.fori_loop(0, n_chunks, chunk_body, 0)


def _hgrn(safe, z, tril):
    b, s, _ = z.shape
    blk = HG_BLOCK
    grid_spec = pltpu.PrefetchScalarGridSpec(
        num_scalar_prefetch=1,
        grid=(b, HG_HEADS, s // blk),
        in_specs=[
            pl.BlockSpec((None, blk, HG_WIDE), lambda bi, h, li, safe_ref: (bi, li, h)),
            pl.BlockSpec(tril.shape, lambda bi, h, li, safe_ref: (0, 0), pipeline_mode=pl.Buffered(1)),
        ],
        out_specs=pl.BlockSpec((None, blk, HG_DV), lambda bi, h, li, safe_ref: (bi, li, h)),
        scratch_shapes=[pltpu.VMEM((HG_DV, HG_DK), jnp.float32)],
    )
    return pl.pallas_call(
        _hgrn_kernel,
        grid_spec=grid_spec,
        out_shape=jax.ShapeDtypeStruct((b, s, HG_HEADS * HG_DV), jnp.bfloat16),
        compiler_params=_params(("parallel", "parallel", "arbitrary")),
        name="hgrn2",
    )(safe, z, tril)


def kernel(x, positions, w_in_e, mla_gq, mla_gkv, w_qb, w_kvb, sgu_ln_g, sgu_ln_b, sgu_w, sgu_b, w_out_e,
           w_in_o, hg_lb, hg_gnorm, w_out_o, ln1_g, ln1_b, w_ff1, w_ff2, ln2_g, ln2_b):
    bsz, seq, d = x.shape
    assert d == D_MODEL and hg_lb.shape[0] == DEPTH == 2
    assert seq % ATTN_TILE == 0 and seq % HG_BLOCK == 0 and (bsz * seq) % TOKEN_TILE == 0
    t = bsz * seq
    bf = jnp.bfloat16
    f32 = jnp.float32
    row = lambda a: a.reshape(1, -1).astype(f32)

    inv_freq = ROPE_BASE ** (-jnp.arange(ROPE_HALF, dtype=f32) / ROPE_HALF)
    freq_col = jnp.broadcast_to(inv_freq[:, None], (ROPE_HALF, LANES))

    we = w_in_e[0]
    s0, s1, s2, s3 = MLA_LORA, 2 * MLA_LORA, 2 * MLA_LORA + MLA_ROPE, 2 * MLA_LORA + MLA_ROPE + SGU_DIM
    kr_cols = jnp.pad(we[:, s1:s2], ((0, 0), (ROPE_LO, HEAD_PAD - ROPE_LO - MLA_ROPE)))
    w_in_p = jnp.concatenate([we[:, :s1], kr_cols, we[:, s2:s3], we[:, s3:]], axis=1).astype(bf)
    assert w_in_p.shape[1] == EVEN_COLS
    dq = MLA_NOPE + MLA_ROPE
    wq_p = jnp.pad(w_qb[0].reshape(MLA_LORA, MLA_HEADS, dq), ((0, 0), (0, 0), (0, HEAD_PAD - dq)))
    wqt_p = wq_p.reshape(MLA_LORA, MLA_HEADS * HEAD_PAD).T.astype(bf)
    wkv = w_kvb[0].reshape(MLA_LORA, MLA_HEADS, MLA_NOPE + MLA_V)
    wk_p = jnp.pad(wkv[:, :, :MLA_NOPE], ((0, 0), (0, 0), (0, HEAD_PAD - MLA_NOPE)))
    wk_p = wk_p.reshape(MLA_LORA, MLA_HEADS * HEAD_PAD).astype(bf)
    wv_p = jnp.pad(wkv[:, :, MLA_NOPE:], ((0, 0), (0, 0), (0, HEAD_PAD - MLA_V)))
    wvt_p = wv_p.reshape(MLA_LORA, MLA_HEADS * HEAD_PAD).T.astype(bf)
    one_col = jnp.tile(jnp.zeros((HEAD_PAD,), f32).at[MLA_V].set(1.0), MLA_HEADS)
    one_col = jnp.broadcast_to(one_col[:, None], (MLA_HEADS * HEAD_PAD, LANES))
    sgu_bias = jnp.repeat(sgu_b[0].T, SGU_GROUP_DIM, axis=1).astype(f32)

    x2 = x.reshape(t, D_MODEL)
    qt, k, vt, b_out = _even_front(x2, positions.reshape(1, t), freq_col, w_in_p, row(mla_gq[0]), row(mla_gkv[0]),
                                   wqt_p, wk_p, wvt_p, one_col, row(sgu_ln_g[0]), row(sgu_ln_b[0]),
                                   sgu_w[0].astype(f32), sgu_bias)
    a_out = _attention(qt, k.reshape(bsz, seq, -1), vt, bsz, seq)
    a_rows = MLA_HEADS * MLA_V
    h1 = _post(x2, [a_out.reshape(t, -1), b_out], [w_out_e[0][:a_rows].astype(bf), w_out_e[0][a_rows:].astype(bf)],
               row(ln1_g[0]), row(ln1_b[0]), w_ff1[0].astype(bf), w_ff2[0].astype(bf), row(ln2_g[0]), row(ln2_b[0]))

    wo = w_in_o[0].reshape(D_MODEL, 4, HG_HEADS, HG_DK).transpose(0, 2, 1, 3).reshape(D_MODEL, 4 * HG_HEADS * HG_DK)
    z, risk = _odd_front(h1, wo.astype(bf), hg_lb.astype(f32), row(hg_gnorm[0]))
    risk = risk[:, :, 0].reshape(t // HG_BLOCK, HG_BLOCK // TOKEN_TILE, HG_HEADS).max(axis=1)
    safe = (risk <= HG_SAFE_EXPONENT).astype(jnp.int32)
    tril = jnp.tril(jnp.ones((HG_CHUNK, HG_CHUNK), bf))
    o = _hgrn(safe, z.reshape(bsz, seq, -1), tril)
    h2 = _post(h1, [o.reshape(t, -1)], [w_out_o[0].astype(bf)], row(ln1_g[1]), row(ln1_b[1]), w_ff1[1].astype(bf),
               w_ff2[1].astype(bf), row(ln2_g[1]), row(ln2_b[1]))
    return h2.reshape(bsz, seq, D_MODEL)
```

```python
import functools
import math

import jax
import jax.numpy as jnp
from jax import lax
from jax.experimental import pallas as pl
from jax.experimental.pallas import tpu as pltpu

D_MODEL = 1024
DEPTH = 2
MLA_HEADS = 8
MLA_LORA = 256
MLA_NOPE = 64
MLA_ROPE = 32
MLA_V = 64
MLA_SCALE = (MLA_NOPE + MLA_ROPE) ** -0.5
LOG2_E = math.log2(math.e)
ROPE_BASE = 10000.0
SGU_GROUPS = 4
SGU_GROUP_DIM = 128
SGU_DIM = SGU_GROUPS * SGU_GROUP_DIM
SGU_CHUNK = 128
HG_HEADS = 8
HG_DK = 128
HG_DV = 128
D_FF = 4 * D_MODEL
DN_ALPHA = (2 * DEPTH) ** 0.25
NORM_EPS = 1e-5

LANES = 128
SUBLANES = 8
VMEM_LIMIT_BYTES = 56 * 1024 * 1024

TOKEN_TILE = 512
ATTN_TILE = 512
HG_CHUNK = 128
HG_BLOCK = 2048
FF_CHUNK = 1024
MASK_VALUE = -1e30
HG_SAFE_EXPONENT = 60.0

HEAD_PAD = LANES
COL_CQ = 0
COL_CKV = COL_CQ + MLA_LORA
COL_KR = COL_CKV + MLA_LORA
COL_U = COL_KR + HEAD_PAD
COL_V = COL_U + SGU_DIM
EVEN_COLS = COL_V + SGU_DIM
ROPE_HALF = MLA_ROPE // 2
ROPE_LO = MLA_NOPE


def _params(semantics):
    return pltpu.CompilerParams(dimension_semantics=semantics, vmem_limit_bytes=VMEM_LIMIT_BYTES)


def _resident(shape):
    zeros = (0,) * len(shape)
    return pl.BlockSpec(shape, lambda *_: zeros, pipeline_mode=pl.Buffered(1))


def _dot(a, b):
    return jnp.dot(a, b, preferred_element_type=jnp.float32)


def _dot_nt(a, b):
    return lax.dot_general(a, b, (((1,), (1,)), ((), ())), preferred_element_type=jnp.float32)


def _sigmoid(x):
    return 0.5 * jnp.tanh(0.5 * x) + 0.5


def _gelu_tanh(x):
    c = math.sqrt(2.0 / math.pi)
    return 0.5 * x * (1.0 + jnp.tanh(c * (x + 0.044715 * (x * x * x))))


def _layer_norm(x, g, b):
    mu = jnp.mean(x, axis=-1, keepdims=True)
    xc = x - mu
    var = jnp.mean(xc * xc, axis=-1, keepdims=True)
    return xc * lax.rsqrt(var + NORM_EPS) * g + b


def _rms_norm(x, g):
    return x * lax.rsqrt(jnp.mean(x * x, axis=-1, keepdims=True) + NORM_EPS) * g


def _even_front_kernel(x_ref, pos_ref, freq_ref, w_in_ref, gq_ref, gkv_ref, wqt_ref, wk_ref, wvt_ref, one_ref,
                       lng_ref, lnb_ref, sw_ref, sb_ref, qt_ref, k_ref, vt_ref, b_ref):
    tm = x_ref.shape[0]
    rep = tm // LANES
    xb = x_ref[...].astype(jnp.bfloat16)
    z = _dot(xb, w_in_ref[...])

    ang = jnp.tile(freq_ref[...], (1, rep)) * pos_ref[...].astype(jnp.float32)
    cos_t = jnp.cos(ang)
    sin_t = jnp.sin(ang)

    cq = _rms_norm(z[:, COL_CQ:COL_CQ + MLA_LORA], gq_ref[...] * (MLA_SCALE * LOG2_E)).astype(jnp.bfloat16)
    ckv = _rms_norm(z[:, COL_CKV:COL_CKV + MLA_LORA], gkv_ref[...]).astype(jnp.bfloat16)

    qt = _dot_nt(wqt_ref[...], cq)
    for h in range(MLA_HEADS):
        base = h * HEAD_PAD
        x1 = qt[base + ROPE_LO:base + ROPE_LO + ROPE_HALF]
        x2 = qt[base + ROPE_LO + ROPE_HALF:base + ROPE_LO + MLA_ROPE]
        slab = jnp.concatenate([qt[base:base + ROPE_LO], x1 * cos_t - x2 * sin_t, x2 * cos_t + x1 * sin_t,
                                qt[base + ROPE_LO + MLA_ROPE:base + HEAD_PAD]], axis=0)
        qt_ref[base:base + HEAD_PAD, :] = slab.astype(qt_ref.dtype)

    vt = _dot_nt(wvt_ref[...], ckv) + jnp.tile(one_ref[...], (1, rep))
    vt_ref[...] = vt.astype(vt_ref.dtype)

    ones = jnp.ones((ROPE_LO, tm), jnp.float32)
    zeros = jnp.zeros((HEAD_PAD - ROPE_LO - MLA_ROPE, tm), jnp.float32)
    cos_k = jnp.concatenate([ones, cos_t, cos_t, zeros], axis=0).T
    sin_k = jnp.concatenate([0.0 * ones, -sin_t, sin_t, zeros], axis=0).T
    lane = lax.broadcasted_iota(jnp.int32, (tm, LANES), 1)
    kr = z[:, COL_KR:COL_KR + HEAD_PAD]
    partner = jnp.where(lane < ROPE_LO + ROPE_HALF, pltpu.roll(kr, LANES - ROPE_HALF, 1), pltpu.roll(kr, ROPE_HALF, 1))
    kr = kr * cos_k + partner * sin_k
    kn = _dot(ckv, wk_ref[...])
    for h in range(MLA_HEADS):
        sl = slice(h * HEAD_PAD, (h + 1) * HEAD_PAD)
        k_ref[:, sl] = (kn[:, sl] + kr).astype(k_ref.dtype)

    u = _gelu_tanh(z[:, COL_U:COL_U + SGU_DIM])
    vn = _layer_norm(_gelu_tanh(z[:, COL_V:COL_V + SGU_DIM]), lng_ref[...], lnb_ref[...]).astype(jnp.bfloat16)
    row = lax.broadcasted_iota(jnp.int32, (SGU_CHUNK, SGU_CHUNK), 0)
    col = lax.broadcasted_iota(jnp.int32, (SGU_CHUNK, SGU_CHUNK), 1)
    causal = col <= row
    bias = sb_ref[...]
    for g in range(SGU_GROUPS):
        wg = jnp.where(causal, sw_ref[g], 0.0).astype(jnp.bfloat16)
        gs = slice(g * SGU_GROUP_DIM, (g + 1) * SGU_GROUP_DIM)
        for c in range(tm // SGU_CHUNK):
            rs = slice(c * SGU_CHUNK, (c + 1) * SGU_CHUNK)
            mixed = _dot(wg, vn[rs, gs]) + bias[:, gs]
            b_ref[rs, gs] = (u[rs, gs] * mixed).astype(b_ref.dtype)


def _even_front(x2, pos_row, freq_col, w_in, gq, gkv, wqt, wk, wvt, one_col, lng, lnb, sw, sb):
    t = x2.shape[0]
    tm = TOKEN_TILE
    row = lambda i: (i, 0)
    col = lambda i: (0, i)
    wide = MLA_HEADS * HEAD_PAD
    out_shapes = (
        jax.ShapeDtypeStruct((wide, t), jnp.bfloat16),
        jax.ShapeDtypeStruct((t, wide), jnp.bfloat16),
        jax.ShapeDtypeStruct((wide, t), jnp.bfloat16),
        jax.ShapeDtypeStruct((t, SGU_DIM), jnp.bfloat16),
    )
    consts = (freq_col, w_in, gq, gkv, wqt, wk, wvt, one_col, lng, lnb, sw, sb)
    return pl.pallas_call(
        _even_front_kernel,
        grid=(t // tm,),
        in_specs=[pl.BlockSpec((tm, D_MODEL), row), pl.BlockSpec((1, tm), col)] + [_resident(a.shape) for a in consts],
        out_specs=(
            pl.BlockSpec((wide, tm), col),
            pl.BlockSpec((tm, wide), row),
            pl.BlockSpec((wide, tm), col),
            pl.BlockSpec((tm, SGU_DIM), row),
        ),
        out_shape=out_shapes,
        compiler_params=_params(("parallel",)),
        name="even_front",
    )(x2, pos_row, *consts)


def _attn_kernel(step_ref, qt_ref, k_ref, vt_ref, o_ref, s_sc, cm_sc, m_sc, acc_sc):
    t = ATTN_TILE
    nq = qt_ref.shape[1] // t
    n_off = nq * (nq - 1) // 2
    assert n_off % 2 == 0 and n_off >= 2
    m_sc[...] = jnp.full(m_sc.shape, MASK_VALUE, jnp.float32)
    acc_sc[...] = jnp.zeros(acc_sc.shape, jnp.float32)

    def tile(i):
        return slice(i * t, (i + 1) * t) if isinstance(i, int) else pl.ds(pl.multiple_of(i * t, t), t)

    def scores(qi, j, slot, hh):
        hs = slice(hh * HEAD_PAD, (hh + 1) * HEAD_PAD)
        s = _dot(k_ref[tile(j), hs], qt_ref[hs, tile(qi)])
        if isinstance(qi, int):
            key = lax.broadcasted_iota(jnp.int32, (t, t), 0)
            qry = lax.broadcasted_iota(jnp.int32, (t, t), 1)
            s = jnp.where(key <= qry, s, MASK_VALUE)
        s_sc[slot, hh] = s
        cm_sc[slot, hh] = jnp.broadcast_to(jnp.max(s, axis=0, keepdims=True), (SUBLANES, t))

    def update(qi, j, slot, hh):
        hs = slice(hh * HEAD_PAD, (hh + 1) * HEAD_PAD)
        m_prev = m_sc[qi, hh]
        m_next = jnp.maximum(m_prev, cm_sc[slot, hh])
        p = jnp.exp2(s_sc[slot, hh] - m_next[0:1, :]).astype(jnp.bfloat16)
        alpha = jnp.exp2(m_prev - m_next)
        acc = alpha[0:1, :] * acc_sc[qi, hh] + _dot(vt_ref[hs, tile(j)], p)
        acc_sc[qi, hh] = acc
        m_sc[qi, hh] = m_next
        return acc

    def off_step(n):
        return step_ref[0, n], step_ref[1, n]

    for hh in range(2):
        scores(*off_step(0), 0, hh)

    def body(p, carry):
        for slot in range(2):
            n = 2 * p + slot
            for hh in range(2):
                scores(*off_step(n + 1), 1 - slot, hh)
                update(*off_step(n), slot, hh)
        return carry

    lax.fori_loop(0, n_off // 2 - 1, body, 0)
    for hh in range(2):
        scores(*off_step(n_off - 1), 1, hh)
        update(*off_step(n_off - 2), 0, hh)
    for hh in range(2):
        scores(0, 0, 0, hh)
        update(*off_step(n_off - 1), 1, hh)

    lane = lax.broadcasted_iota(jnp.int32, (t, LANES), 1)
    for qi in range(nq):
        slot = qi % 2
        outs = []
        for hh in range(2):
            if qi + 1 < nq:
                scores(qi + 1, qi + 1, 1 - slot, hh)
            acc = update(qi, qi, slot, hh)
            outs.append((acc / acc[MLA_V:MLA_V + 1, :]).T)
        pair = jnp.where(lane < MLA_V, outs[0], pltpu.roll(outs[1], MLA_V, 1))
        o_ref[qi * t:(qi + 1) * t, :] = pair.astype(o_ref.dtype)


def _attention(qt, k, vt, bsz, seq):
    t = ATTN_TILE
    nq = seq // t
    pairs = MLA_HEADS // 2
    steps = jnp.asarray([[qi for qi in range(nq) for _ in range(qi)],
                         [j for qi in range(nq) for j in range(qi)]], jnp.int32)
    grid_spec = pltpu.PrefetchScalarGridSpec(
        num_scalar_prefetch=1,
        grid=(bsz, pairs),
        in_specs=[
            pl.BlockSpec((2 * HEAD_PAD, seq), lambda bi, p, steps_ref: (p, bi)),
            pl.BlockSpec((None, seq, 2 * HEAD_PAD), lambda bi, p, steps_ref: (bi, 0, p)),
            pl.BlockSpec((2 * HEAD_PAD, seq), lambda bi, p, steps_ref: (p, bi)),
        ],
        out_specs=pl.BlockSpec((None, seq, 2 * MLA_V), lambda bi, p, steps_ref: (bi, 0, p)),
        scratch_shapes=[
            pltpu.VMEM((2, 2, t, t), jnp.float32),
            pltpu.VMEM((2, 2, SUBLANES, t), jnp.float32),
            pltpu.VMEM((nq, 2, SUBLANES, t), jnp.float32),
            pltpu.VMEM((nq, 2, HEAD_PAD, t), jnp.float32),
        ],
    )
    return pl.pallas_call(
        _attn_kernel,
        grid_spec=grid_spec,
        out_shape=jax.ShapeDtypeStruct((bsz, seq, MLA_HEADS * MLA_V), jnp.bfloat16),
        compiler_params=_params(("parallel", "parallel")),
        name="mla_attention",
    )(steps, qt, k, vt)


def _post_kernel(n_mix, *refs):
    h_ref = refs[0]
    mix_refs = refs[1:1 + n_mix]
    wout_refs = refs[1 + n_mix:1 + 2 * n_mix]
    g1_ref, b1_ref, w1_ref, w2_ref, g2_ref, b2_ref, o_ref = refs[1 + 2 * n_mix:]
    tm = h_ref.shape[0]
    halves = [slice(i * (tm // 2), (i + 1) * (tm // 2)) for i in range(2)]
    mixes = []
    for rows in halves:
        mix = _dot(mix_refs[0][rows, :], wout_refs[0][...])
        for m_ref, w_ref in zip(mix_refs[1:], wout_refs[1:]):
            mix = mix + _dot(m_ref[rows, :], w_ref[...])
        mixes.append(mix)
    for rows, mix in zip(halves, mixes):
        y = _layer_norm(DN_ALPHA * h_ref[rows, :] + mix, g1_ref[...], b1_ref[...])
        yb = y.astype(jnp.bfloat16)
        ff = None
        for c in range(D_FF // FF_CHUNK):
            cs = slice(c * FF_CHUNK, (c + 1) * FF_CHUNK)
            a = jnp.maximum(_dot(yb, w1_ref[:, cs]), 0.0)
            part = _dot((a * a).astype(jnp.bfloat16), w2_ref[cs, :])
            ff = part if ff is None else ff + part
        o_ref[rows, :] = _layer_norm(DN_ALPHA * y + ff, g2_ref[...], b2_ref[...])


def _post(layer, h2, mixes, wouts, g1, b1, w1, w2, g2, b2):
    t = h2.shape[0]
    tm = TOKEN_TILE
    row = lambda i: (i, 0)
    n = len(mixes)
    layer_slab = lambda a: pl.BlockSpec((None,) + a.shape[1:], lambda i: (layer, 0, 0), pipeline_mode=pl.Buffered(1))
    in_specs = [pl.BlockSpec((tm, D_MODEL), row)]
    in_specs += [pl.BlockSpec((tm, m.shape[1]), row) for m in mixes]
    in_specs += [_resident(w.shape) for w in wouts]
    in_specs += [_resident(g1.shape), _resident(b1.shape), layer_slab(w1), layer_slab(w2), _resident(g2.shape),
                 _resident(b2.shape)]
    return pl.pallas_call(
        functools.partial(_post_kernel, n),
        grid=(t // tm,),
        in_specs=in_specs,
        out_specs=pl.BlockSpec((tm, D_MODEL), row),
        out_shape=jax.ShapeDtypeStruct((t, D_MODEL), jnp.float32),
        compiler_params=_params(("parallel",)),
        name="post",
    )(h2, *mixes, *wouts, g1, b1, w1, w2, g2, b2)


HG_SLABS = ("q", "k", "v", "gate")
HG_SLAB = {name: slice(i * HG_DK, (i + 1) * HG_DK) for i, name in enumerate(HG_SLABS)}
HG_WIDE = len(HG_SLABS) * HG_DK


def _odd_front_kernel(x_ref, w_ref, lb_ref, gn_ref, o_ref, lg_ref, risk_ref):
    tm = x_ref.shape[0]
    half = HG_CHUNK // 2
    xb = x_ref[...].astype(jnp.bfloat16)
    lbp = lb_ref[...]
    mx = jnp.max(lbp, axis=0, keepdims=True)
    e = jnp.exp(lbp - mx)
    sm = e / jnp.sum(e, axis=0, keepdims=True)
    lb_all = (sm[0:1, :] + sm[1:2, :]) - sm[0:1, :]
    for h in range(HG_HEADS):
        hs = slice(h * HG_DK, (h + 1) * HG_DK)
        z = _dot(xb, w_ref[:, h * 4 * HG_DK:(h + 1) * 4 * HG_DK])
        zq, zf, zi, zg = (z[:, i * HG_DK:(i + 1) * HG_DK] for i in range(4))
        lb = lb_all[:, hs]
        gate = lb + (1.0 - lb) * _sigmoid(zf)
        lg = jnp.log(gate)
        lg_ref[:, hs] = lg
        base = h * HG_WIDE
        slabs = {"q": zq * _sigmoid(zq), "k": 1.0 - gate, "v": zi, "gate": gn_ref[:, hs] * (zg * _sigmoid(zg))}
        for name, val in slabs.items():
            sl = HG_SLAB[name]
            o_ref[:, base + sl.start:base + sl.stop] = val.astype(o_ref.dtype)
        risk = None
        for r0 in range(0, tm, half):
            s = -jnp.sum(lg[r0:r0 + half], axis=0, keepdims=True)
            risk = s if risk is None else jnp.maximum(risk, s)
        risk_ref[h:h + 1, :] = jnp.broadcast_to(jnp.max(risk, axis=1, keepdims=True), (1, LANES))


def _odd_front(x2, w, lb, gn):
    t = x2.shape[0]
    tm = TOKEN_TILE
    return pl.pallas_call(
        _odd_front_kernel,
        grid=(t // tm,),
        in_specs=[pl.BlockSpec((tm, x2.shape[1]), lambda i: (i, 0)), _resident(w.shape), _resident(lb.shape),
                  _resident(gn.shape)],
        out_specs=(pl.BlockSpec((tm, HG_HEADS * HG_WIDE), lambda i: (i, 0)),
                   pl.BlockSpec((tm, HG_HEADS * HG_DK), lambda i: (i, 0)),
                   pl.BlockSpec((None, HG_HEADS, LANES), lambda i: (i, 0, 0))),
        out_shape=(jax.ShapeDtypeStruct((t, HG_HEADS * HG_WIDE), jnp.bfloat16),
                   jax.ShapeDtypeStruct((t, HG_HEADS * HG_DK), jnp.float32),
                   jax.ShapeDtypeStruct((t // tm, HG_HEADS, LANES), jnp.float32)),
        compiler_params=_params(("parallel",)),
        name="odd_front",
    )(x2, w, lb, gn)


def _midpoint_rows(b, half):
    c = b.shape[0]
    if half >= SUBLANES:
        parts = []
        for p in range(c // (2 * half)):
            r = p * 2 * half + half - 1
            parts.append(jnp.broadcast_to(b[r:r + 1, :], (2 * half, LANES)))
        return parts[0] if len(parts) == 1 else jnp.concatenate(parts, axis=0)
    b3 = b.reshape(c // SUBLANES, SUBLANES, LANES)
    sub = lax.broadcasted_iota(jnp.int32, b3.shape, 1)
    out = None
    for p in range(SUBLANES // (2 * half)):
        r = p * 2 * half + half - 1
        cand = jnp.broadcast_to(b3[:, r:r + 1, :], b3.shape)
        out = cand if out is None else jnp.where(sub >= p * 2 * half, cand, out)
    return out.reshape(c, LANES)


def _hgrn_kernel(safe_ref, z_ref, lg_ref, tril_ref, o_ref, st_ref):
    c = HG_CHUNK
    mid = c // 2 - 1
    n_chunks = z_ref.shape[0] // c
    n_levels = int(math.log2(c))
    bi, h, li = pl.program_id(0), pl.program_id(1), pl.program_id(2)
    safe = safe_ref[bi * pl.num_programs(2) + li, h] != 0

    @pl.when(li == 0)
    def _():
        st_ref[...] = jnp.zeros(st_ref.shape, jnp.float32)

    tril = tril_ref[...]

    def slab(rs, name):
        return z_ref[rs, HG_SLAB[name]]

    def cumsum(rs):
        lg = lg_ref[rs, :]
        hi = lg.astype(jnp.bfloat16)
        r1 = lg - hi.astype(jnp.float32)
        md = r1.astype(jnp.bfloat16)
        lo = (r1 - md.astype(jnp.float32)).astype(jnp.bfloat16)
        return (_dot(tril, hi) + _dot(tril, md)) + _dot(tril, lo)

    def load_qk(rs):
        return slab(rs, "q").astype(jnp.float32), slab(rs, "k").astype(jnp.float32)

    t_idx = lax.broadcasted_iota(jnp.int32, (c, c), 0)
    s_idx = lax.broadcasted_iota(jnp.int32, (c, c), 1)

    def state_update(rs, kd, b_last):
        upd = lax.dot_general(slab(rs, "v"), kd, (((0,), (0,)), ((), ())), preferred_element_type=jnp.float32)
        return jnp.exp(b_last), upd

    def chunk_output(rs, qd, st, attn):
        o = _dot_nt(qd, st.astype(jnp.bfloat16))
        o = o + _dot(attn.astype(jnp.bfloat16), slab(rs, "v"))
        o = o * lax.rsqrt(jnp.mean(o * o, axis=-1, keepdims=True) + NORM_EPS)
        o_ref[rs, :] = (o * slab(rs, "gate").astype(jnp.float32)).astype(o_ref.dtype)

    @pl.when(safe)
    def _():
        causal = s_idx <= t_idx
        chunks = [slice(ci * c, (ci + 1) * c) for ci in range(n_chunks)]
        bs = [cumsum(rs) for rs in chunks]
        scores, decays, upds, qds = [], [], [], []
        for rs, b in zip(chunks, bs):
            q, k = load_qk(rs)
            b_mid, b_last = b[mid:mid + 1, :], b[c - 1:c, :]
            d = b - b_mid
            qe = q * jnp.exp(d)
            ke = k * jnp.exp(-d)
            scores.append(_dot_nt(qe.astype(jnp.bfloat16), ke.astype(jnp.bfloat16)))
            decay, upd = state_update(rs, (ke * jnp.exp(b_last - b_mid)).astype(jnp.bfloat16), b_last)
            decays.append(decay)
            upds.append(upd)
            qds.append((qe * jnp.exp(b_mid)).astype(jnp.bfloat16))
        states = [st_ref[...]]
        for ci in range(n_chunks):
            states.append(states[-1] * decays[ci] + upds[ci])
        st_ref[...] = states[-1]
        for ci, rs in enumerate(chunks):
            chunk_output(rs, qds[ci], states[ci], jnp.where(causal, scores[ci], 0.0))

    @pl.when(jnp.logical_not(safe))
    def _():
        x = jnp.bitwise_xor(t_idx, s_idx)
        level = jnp.zeros((c, c), jnp.int32)
        for bit in range(n_levels):
            level = level + (x >= (1 << bit)).astype(jnp.int32)
        level = jnp.where(s_idx <= t_idx, level, -1)
        row_id = lax.broadcasted_iota(jnp.int32, (c, LANES), 0)

        def chunk_body(ci, carry):
            rs = pl.ds(pl.multiple_of(ci * c, c), c)
            (q, k), b = load_qk(rs), cumsum(rs)
            b_last = b[c - 1:c, :]
            attn = jnp.where(level == 0, _dot_nt(q.astype(jnp.bfloat16), k.astype(jnp.bfloat16)), 0.0)
            for lv in range(1, n_levels + 1):
                half = 1 << (lv - 1)
                right = jnp.bitwise_and(row_id, half) != 0
                d = b - _midpoint_rows(b, half)
                decay = jnp.exp(jnp.where(right, d, -d))
                zl = (jnp.where(right, q, k) * decay).astype(jnp.bfloat16)
                attn = jnp.where(level == lv, _dot_nt(zl, zl), attn)
            st = st_ref[...]
            decay, upd = state_update(rs, (k * jnp.exp(b_last - b)).astype(jnp.bfloat16), b_last)
            chunk_output(rs, (q * jnp.exp(b)).astype(jnp.bfloat16), st, attn)
            st_ref[...] = st * decay + upd
            return carry

        lax.fori_loop(0, n_chunks, chunk_body, 0)


def _hgrn(safe, z, lg, tril):
    b, s, _ = z.shape
    blk = HG_BLOCK
    grid_spec = pltpu.PrefetchScalarGridSpec(
        num_scalar_prefetch=1,
        grid=(b, HG_HEADS, s // blk),
        in_specs=[
            pl.BlockSpec((None, blk, HG_WIDE), lambda bi, h, li, safe_ref: (bi, li, h)),
            pl.BlockSpec((None, blk, HG_DK), lambda bi, h, li, safe_ref: (bi, li, h)),
            pl.BlockSpec(tril.shape, lambda bi, h, li, safe_ref: (0, 0), pipeline_mode=pl.Buffered(1)),
        ],
        out_specs=pl.BlockSpec((None, blk, HG_DV), lambda bi, h, li, safe_ref: (bi, li, h)),
        scratch_shapes=[pltpu.VMEM((HG_DV, HG_DK), jnp.float32)],
    )
    return pl.pallas_call(
        _hgrn_kernel,
        grid_spec=grid_spec,
        out_shape=jax.ShapeDtypeStruct((b, s, HG_HEADS * HG_DV), jnp.bfloat16),
        compiler_params=_params(("parallel", "parallel", "arbitrary")),
        name="hgrn2",
    )(safe, z, lg, tril)


def kernel(x, positions, w_in_e, mla_gq, mla_gkv, w_qb, w_kvb, sgu_ln_g, sgu_ln_b, sgu_w, sgu_b, w_out_e,
           w_in_o, hg_lb, hg_gnorm, w_out_o, ln1_g, ln1_b, w_ff1, w_ff2, ln2_g, ln2_b):
    bsz, seq, d = x.shape
    assert d == D_MODEL and hg_lb.shape[0] == DEPTH == 2
    assert seq % ATTN_TILE == 0 and seq % HG_BLOCK == 0 and (bsz * seq) % TOKEN_TILE == 0
    t = bsz * seq
    bf = jnp.bfloat16
    f32 = jnp.float32
    row = lambda a: a.reshape(1, -1).astype(f32)

    inv_freq = ROPE_BASE ** (-jnp.arange(ROPE_HALF, dtype=f32) / ROPE_HALF)
    freq_col = jnp.broadcast_to(inv_freq[:, None], (ROPE_HALF, LANES))

    we = w_in_e[0]
    s0, s1, s2, s3 = MLA_LORA, 2 * MLA_LORA, 2 * MLA_LORA + MLA_ROPE, 2 * MLA_LORA + MLA_ROPE + SGU_DIM
    kr_cols = jnp.pad(we[:, s1:s2], ((0, 0), (ROPE_LO, HEAD_PAD - ROPE_LO - MLA_ROPE)))
    w_in_p = jnp.concatenate([we[:, :s1], kr_cols, we[:, s2:s3], we[:, s3:]], axis=1).astype(bf)
    assert w_in_p.shape[1] == EVEN_COLS
    dq = MLA_NOPE + MLA_ROPE
    wq_p = jnp.pad(w_qb[0].reshape(MLA_LORA, MLA_HEADS, dq), ((0, 0), (0, 0), (0, HEAD_PAD - dq)))
    wqt_p = wq_p.reshape(MLA_LORA, MLA_HEADS * HEAD_PAD).T.astype(bf)
    wkv = w_kvb[0].reshape(MLA_LORA, MLA_HEADS, MLA_NOPE + MLA_V)
    wk_p = jnp.pad(wkv[:, :, :MLA_NOPE], ((0, 0), (0, 0), (0, HEAD_PAD - MLA_NOPE)))
    wk_p = wk_p.reshape(MLA_LORA, MLA_HEADS * HEAD_PAD).astype(bf)
    wv_p = jnp.pad(wkv[:, :, MLA_NOPE:], ((0, 0), (0, 0), (0, HEAD_PAD - MLA_V)))
    wvt_p = wv_p.reshape(MLA_LORA, MLA_HEADS * HEAD_PAD).T.astype(bf)
    one_col = jnp.tile(jnp.zeros((HEAD_PAD,), f32).at[MLA_V].set(1.0), MLA_HEADS)
    one_col = jnp.broadcast_to(one_col[:, None], (MLA_HEADS * HEAD_PAD, LANES))
    sgu_bias = jnp.repeat(sgu_b[0].T, SGU_GROUP_DIM, axis=1).astype(f32)

    x2 = x.reshape(t, D_MODEL)
    qt, k, vt, b_out = _even_front(x2, positions.reshape(1, t), freq_col, w_in_p, row(mla_gq[0]), row(mla_gkv[0]),
                                   wqt_p, wk_p, wvt_p, one_col, row(sgu_ln_g[0]), row(sgu_ln_b[0]),
                                   sgu_w[0].astype(f32), sgu_bias)
    a_out = _attention(qt, k.reshape(bsz, seq, -1), vt, bsz, seq)
    a_rows = MLA_HEADS * MLA_V
    w1_bf, w2_bf = w_ff1.astype(bf), w_ff2.astype(bf)
    h1 = _post(0, x2, [a_out.reshape(t, -1), b_out], [w_out_e[0][:a_rows].astype(bf), w_out_e[0][a_rows:].astype(bf)],
               row(ln1_g[0]), row(ln1_b[0]), w1_bf, w2_bf, row(ln2_g[0]), row(ln2_b[0]))

    wo = w_in_o[0].reshape(D_MODEL, 4, HG_HEADS, HG_DK).transpose(0, 2, 1, 3).reshape(D_MODEL, 4 * HG_HEADS * HG_DK)
    z, lg, risk = _odd_front(h1, wo.astype(bf), hg_lb.astype(f32), row(hg_gnorm[0]))
    risk = risk[:, :, 0].reshape(t // HG_BLOCK, HG_BLOCK // TOKEN_TILE, HG_HEADS).max(axis=1)
    safe = (risk <= HG_SAFE_EXPONENT).astype(jnp.int32)
    tril = jnp.tril(jnp.ones((HG_CHUNK, HG_CHUNK), bf))
    o = _hgrn(safe, z.reshape(bsz, seq, -1), lg.reshape(bsz, seq, -1), tril)
    h2 = _post(1, h1, [o.reshape(t, -1)], [w_out_o[0].astype(bf)], row(ln1_g[1]), row(ln1_b[1]), w1_bf, w2_bf,
               row(ln2_g[1]), row(ln2_b[1]))
    return h2.reshape(bsz, seq, D_MODEL)
```

```python
import functools
import math

import jax
import jax.numpy as jnp
from jax import lax
from jax.experimental import pallas as pl
from jax.experimental.pallas import tpu as pltpu

D_MODEL = 1024
DEPTH = 2
MLA_HEADS = 8
MLA_LORA = 256
MLA_NOPE = 64
MLA_ROPE = 32
MLA_V = 64
MLA_SCALE = (MLA_NOPE + MLA_ROPE) ** -0.5
LOG2_E = math.log2(math.e)
ROPE_BASE = 10000.0
SGU_GROUPS = 4
SGU_GROUP_DIM = 128
SGU_DIM = SGU_GROUPS * SGU_GROUP_DIM
SGU_CHUNK = 128
HG_HEADS = 8
HG_DK = 128
HG_DV = 128
D_FF = 4 * D_MODEL
DN_ALPHA = (2 * DEPTH) ** 0.25
NORM_EPS = 1e-5

LANES = 128
SUBLANES = 8
VMEM_LIMIT_BYTES = 56 * 1024 * 1024

TOKEN_TILE = 512
POST_TILE = 1024
POST_CHAIN_ROWS = 256
ATTN_TILE = 512
HG_CHUNK = 128
HG_BLOCK = 2048
FF_CHUNK = 1024
MASK_VALUE = -1e30
HG_SAFE_EXPONENT = 60.0

HEAD_PAD = LANES
COL_CQ = 0
COL_CKV = COL_CQ + MLA_LORA
COL_KR = COL_CKV + MLA_LORA
COL_U = COL_KR + HEAD_PAD
COL_V = COL_U + SGU_DIM
EVEN_COLS = COL_V + SGU_DIM
ROPE_HALF = MLA_ROPE // 2
ROPE_LO = MLA_NOPE


def _params(semantics):
    return pltpu.CompilerParams(dimension_semantics=semantics, vmem_limit_bytes=VMEM_LIMIT_BYTES)


def _resident(shape):
    zeros = (0,) * len(shape)
    return pl.BlockSpec(shape, lambda *_: zeros, pipeline_mode=pl.Buffered(1))


def _dot(a, b):
    return jnp.dot(a, b, preferred_element_type=jnp.float32)


def _dot_nt(a, b):
    return lax.dot_general(a, b, (((1,), (1,)), ((), ())), preferred_element_type=jnp.float32)


def _sigmoid(x):
    return 0.5 * jnp.tanh(0.5 * x) + 0.5


def _gelu_tanh(x):
    c = math.sqrt(2.0 / math.pi)
    return 0.5 * x * (1.0 + jnp.tanh(c * (x + 0.044715 * (x * x * x))))


def _layer_norm(x, g, b):
    mu = jnp.mean(x, axis=-1, keepdims=True)
    xc = x - mu
    var = jnp.mean(xc * xc, axis=-1, keepdims=True)
    return xc * lax.rsqrt(var + NORM_EPS) * g + b


def _rms_norm(x, g):
    return x * lax.rsqrt(jnp.mean(x * x, axis=-1, keepdims=True) + NORM_EPS) * g


def _even_front_kernel(x_ref, pos_ref, freq_ref, w_in_ref, gq_ref, gkv_ref, wqt_ref, wk_ref, wvt_ref, one_ref,
                       lng_ref, lnb_ref, sw_ref, sb_ref, qt_ref, k_ref, vt_ref, b_ref):
    tm = x_ref.shape[0]
    rep = tm // LANES
    xb = x_ref[...].astype(jnp.bfloat16)
    z = _dot(xb, w_in_ref[...])

    ang = jnp.tile(freq_ref[...], (1, rep)) * pos_ref[...].astype(jnp.float32)
    cos_t = jnp.cos(ang)
    sin_t = jnp.sin(ang)

    cq = _rms_norm(z[:, COL_CQ:COL_CQ + MLA_LORA], gq_ref[...] * (MLA_SCALE * LOG2_E)).astype(jnp.bfloat16)
    ckv = _rms_norm(z[:, COL_CKV:COL_CKV + MLA_LORA], gkv_ref[...]).astype(jnp.bfloat16)

    qt = _dot_nt(wqt_ref[...], cq)
    for h in range(MLA_HEADS):
        base = h * HEAD_PAD
        x1 = qt[base + ROPE_LO:base + ROPE_LO + ROPE_HALF]
        x2 = qt[base + ROPE_LO + ROPE_HALF:base + ROPE_LO + MLA_ROPE]
        slab = jnp.concatenate([qt[base:base + ROPE_LO], x1 * cos_t - x2 * sin_t, x2 * cos_t + x1 * sin_t,
                                qt[base + ROPE_LO + MLA_ROPE:base + HEAD_PAD]], axis=0)
        qt_ref[base:base + HEAD_PAD, :] = slab.astype(qt_ref.dtype)

    vt = _dot_nt(wvt_ref[...], ckv) + jnp.tile(one_ref[...], (1, rep))
    vt_ref[...] = vt.astype(vt_ref.dtype)

    ones = jnp.ones((ROPE_LO, tm), jnp.float32)
    zeros = jnp.zeros((HEAD_PAD - ROPE_LO - MLA_ROPE, tm), jnp.float32)
    cos_k = jnp.concatenate([ones, cos_t, cos_t, zeros], axis=0).T
    sin_k = jnp.concatenate([0.0 * ones, -sin_t, sin_t, zeros], axis=0).T
    lane = lax.broadcasted_iota(jnp.int32, (tm, LANES), 1)
    kr = z[:, COL_KR:COL_KR + HEAD_PAD]
    partner = jnp.where(lane < ROPE_LO + ROPE_HALF, pltpu.roll(kr, LANES - ROPE_HALF, 1), pltpu.roll(kr, ROPE_HALF, 1))
    kr = kr * cos_k + partner * sin_k
    kn = _dot(ckv, wk_ref[...])
    for h in range(MLA_HEADS):
        sl = slice(h * HEAD_PAD, (h + 1) * HEAD_PAD)
        k_ref[:, sl] = (kn[:, sl] + kr).astype(k_ref.dtype)

    u = _gelu_tanh(z[:, COL_U:COL_U + SGU_DIM])
    vn = _layer_norm(_gelu_tanh(z[:, COL_V:COL_V + SGU_DIM]), lng_ref[...], lnb_ref[...]).astype(jnp.bfloat16)
    row = lax.broadcasted_iota(jnp.int32, (SGU_CHUNK, SGU_CHUNK), 0)
    col = lax.broadcasted_iota(jnp.int32, (SGU_CHUNK, SGU_CHUNK), 1)
    causal = col <= row
    bias = sb_ref[...]
    for g in range(SGU_GROUPS):
        wg = jnp.where(causal, sw_ref[g], 0.0).astype(jnp.bfloat16)
        gs = slice(g * SGU_GROUP_DIM, (g + 1) * SGU_GROUP_DIM)
        for c in range(tm // SGU_CHUNK):
            rs = slice(c * SGU_CHUNK, (c + 1) * SGU_CHUNK)
            mixed = _dot(wg, vn[rs, gs]) + bias[:, gs]
            b_ref[rs, gs] = (u[rs, gs] * mixed).astype(b_ref.dtype)


def _even_front(x2, pos_row, freq_col, w_in, gq, gkv, wqt, wk, wvt, one_col, lng, lnb, sw, sb):
    t = x2.shape[0]
    tm = TOKEN_TILE
    row = lambda i: (i, 0)
    col = lambda i: (0, i)
    wide = MLA_HEADS * HEAD_PAD
    out_shapes = (
        jax.ShapeDtypeStruct((wide, t), jnp.bfloat16),
        jax.ShapeDtypeStruct((t, wide), jnp.bfloat16),
        jax.ShapeDtypeStruct((wide, t), jnp.bfloat16),
        jax.ShapeDtypeStruct((t, SGU_DIM), jnp.bfloat16),
    )
    consts = (freq_col, w_in, gq, gkv, wqt, wk, wvt, one_col, lng, lnb, sw, sb)
    return pl.pallas_call(
        _even_front_kernel,
        grid=(t // tm,),
        in_specs=[pl.BlockSpec((tm, D_MODEL), row), pl.BlockSpec((1, tm), col)] + [_resident(a.shape) for a in consts],
        out_specs=(
            pl.BlockSpec((wide, tm), col),
            pl.BlockSpec((tm, wide), row),
            pl.BlockSpec((wide, tm), col),
            pl.BlockSpec((tm, SGU_DIM), row),
        ),
        out_shape=out_shapes,
        compiler_params=_params(("parallel",)),
        name="even_front",
    )(x2, pos_row, *consts)


def _attn_kernel(step_ref, qt_ref, k_ref, vt_ref, o_ref, s_sc, cm_sc, m_sc, acc_sc):
    t = ATTN_TILE
    nq = qt_ref.shape[1] // t
    n_off = nq * (nq - 1) // 2
    assert n_off % 2 == 0 and n_off >= 2
    m_sc[...] = jnp.full(m_sc.shape, MASK_VALUE, jnp.float32)
    acc_sc[...] = jnp.zeros(acc_sc.shape, jnp.float32)

    def tile(i):
        return slice(i * t, (i + 1) * t) if isinstance(i, int) else pl.ds(pl.multiple_of(i * t, t), t)

    def scores(qi, j, slot, hh):
        hs = slice(hh * HEAD_PAD, (hh + 1) * HEAD_PAD)
        s = _dot(k_ref[tile(j), hs], qt_ref[hs, tile(qi)])
        if isinstance(qi, int):
            key = lax.broadcasted_iota(jnp.int32, (t, t), 0)
            qry = lax.broadcasted_iota(jnp.int32, (t, t), 1)
            s = jnp.where(key <= qry, s, MASK_VALUE)
        s_sc[slot, hh] = s
        cm_sc[slot, hh] = jnp.broadcast_to(jnp.max(s, axis=0, keepdims=True), (SUBLANES, t))

    def update(qi, j, slot, hh):
        hs = slice(hh * HEAD_PAD, (hh + 1) * HEAD_PAD)
        m_prev = m_sc[qi, hh]
        m_next = jnp.maximum(m_prev, cm_sc[slot, hh])
        p = jnp.exp2(s_sc[slot, hh] - m_next[0:1, :]).astype(jnp.bfloat16)
        alpha = jnp.exp2(m_prev - m_next)
        acc = alpha[0:1, :] * acc_sc[qi, hh] + _dot(vt_ref[hs, tile(j)], p)
        acc_sc[qi, hh] = acc
        m_sc[qi, hh] = m_next
        return acc

    def off_step(n):
        return step_ref[0, n], step_ref[1, n]

    for hh in range(2):
        scores(*off_step(0), 0, hh)

    def body(p, carry):
        for slot in range(2):
            n = 2 * p + slot
            for hh in range(2):
                scores(*off_step(n + 1), 1 - slot, hh)
                update(*off_step(n), slot, hh)
        return carry

    lax.fori_loop(0, n_off // 2 - 1, body, 0)
    for hh in range(2):
        scores(*off_step(n_off - 1), 1, hh)
        update(*off_step(n_off - 2), 0, hh)
    for hh in range(2):
        scores(0, 0, 0, hh)
        update(*off_step(n_off - 1), 1, hh)

    lane = lax.broadcasted_iota(jnp.int32, (t, LANES), 1)
    for qi in range(nq):
        slot = qi % 2
        outs = []
        for hh in range(2):
            if qi + 1 < nq:
                scores(qi + 1, qi + 1, 1 - slot, hh)
            acc = update(qi, qi, slot, hh)
            outs.append((acc / acc[MLA_V:MLA_V + 1, :]).T)
        pair = jnp.where(lane < MLA_V, outs[0], pltpu.roll(outs[1], MLA_V, 1))
        o_ref[qi * t:(qi + 1) * t, :] = pair.astype(o_ref.dtype)


def _attention(qt, k, vt, bsz, seq):
    t = ATTN_TILE
    nq = seq // t
    pairs = MLA_HEADS // 2
    steps = jnp.asarray([[qi for qi in range(nq) for _ in range(qi)],
                         [j for qi in range(nq) for j in range(qi)]], jnp.int32)
    grid_spec = pltpu.PrefetchScalarGridSpec(
        num_scalar_prefetch=1,
        grid=(bsz, pairs),
        in_specs=[
            pl.BlockSpec((2 * HEAD_PAD, seq), lambda bi, p, steps_ref: (p, bi)),
            pl.BlockSpec((None, seq, 2 * HEAD_PAD), lambda bi, p, steps_ref: (bi, 0, p)),
            pl.BlockSpec((2 * HEAD_PAD, seq), lambda bi, p, steps_ref: (p, bi)),
        ],
        out_specs=pl.BlockSpec((None, seq, 2 * MLA_V), lambda bi, p, steps_ref: (bi, 0, p)),
        scratch_shapes=[
            pltpu.VMEM((2, 2, t, t), jnp.float32),
            pltpu.VMEM((2, 2, SUBLANES, t), jnp.float32),
            pltpu.VMEM((nq, 2, SUBLANES, t), jnp.float32),
            pltpu.VMEM((nq, 2, HEAD_PAD, t), jnp.float32),
        ],
    )
    return pl.pallas_call(
        _attn_kernel,
        grid_spec=grid_spec,
        out_shape=jax.ShapeDtypeStruct((bsz, seq, MLA_HEADS * MLA_V), jnp.bfloat16),
        compiler_params=_params(("parallel", "parallel")),
        name="mla_attention",
    )(steps, qt, k, vt)


def _post_kernel(n_mix, *refs):
    h_ref = refs[0]
    mix_refs = refs[1:1 + n_mix]
    wout_refs = refs[1 + n_mix:1 + 2 * n_mix]
    g1_ref, b1_ref, w1_ref, w2_ref, g2_ref, b2_ref, o_ref = refs[1 + 2 * n_mix:]
    tm = h_ref.shape[0]
    halves = [slice(r, r + POST_CHAIN_ROWS) for r in range(0, tm, POST_CHAIN_ROWS)]
    mixes = []
    for rows in halves:
        mix = _dot(mix_refs[0][rows, :], wout_refs[0][...])
        for m_ref, w_ref in zip(mix_refs[1:], wout_refs[1:]):
            mix = mix + _dot(m_ref[rows, :], w_ref[...])
        mixes.append(mix)
    for rows, mix in zip(halves, mixes):
        y = _layer_norm(DN_ALPHA * h_ref[rows, :] + mix, g1_ref[...], b1_ref[...])
        yb = y.astype(jnp.bfloat16)
        ff = None
        for c in range(D_FF // FF_CHUNK):
            cs = slice(c * FF_CHUNK, (c + 1) * FF_CHUNK)
            a = jnp.maximum(_dot(yb, w1_ref[:, cs]), 0.0)
            part = _dot((a * a).astype(jnp.bfloat16), w2_ref[cs, :])
            ff = part if ff is None else ff + part
        o_ref[rows, :] = _layer_norm(DN_ALPHA * y + ff, g2_ref[...], b2_ref[...])


def _post(layer, h2, mixes, wouts, g1, b1, w1, w2, g2, b2):
    t = h2.shape[0]
    tm = POST_TILE
    row = lambda i: (i, 0)
    n = len(mixes)
    layer_slab = lambda a: pl.BlockSpec((None,) + a.shape[1:], lambda i: (layer, 0, 0), pipeline_mode=pl.Buffered(1))
    in_specs = [pl.BlockSpec((tm, D_MODEL), row)]
    in_specs += [pl.BlockSpec((tm, m.shape[1]), row) for m in mixes]
    in_specs += [_resident(w.shape) for w in wouts]
    in_specs += [_resident(g1.shape), _resident(b1.shape), layer_slab(w1), layer_slab(w2), _resident(g2.shape),
                 _resident(b2.shape)]
    return pl.pallas_call(
        functools.partial(_post_kernel, n),
        grid=(t // tm,),
        in_specs=in_specs,
        out_specs=pl.BlockSpec((tm, D_MODEL), row),
        out_shape=jax.ShapeDtypeStruct((t, D_MODEL), jnp.float32),
        compiler_params=_params(("parallel",)),
        name="post",
    )(h2, *mixes, *wouts, g1, b1, w1, w2, g2, b2)


HG_SLABS = ("q", "k", "gate", "lg2")
HG_SLAB = {name: slice(i * HG_DK, (i + 1) * HG_DK) for i, name in enumerate(HG_SLABS)}
HG_WIDE = len(HG_SLABS) * HG_DK


def _odd_front_kernel(x_ref, w_ref, lb_ref, gn_ref, o_ref, v_ref, risk_ref):
    tm = x_ref.shape[0]
    half = HG_CHUNK // 2
    xb = x_ref[...].astype(jnp.bfloat16)
    lbp = lb_ref[...]
    mx = jnp.max(lbp, axis=0, keepdims=True)
    e = jnp.exp(lbp - mx)
    sm = e / jnp.sum(e, axis=0, keepdims=True)
    lb_all = (sm[0:1, :] + sm[1:2, :]) - sm[0:1, :]
    for h in range(HG_HEADS):
        hs = slice(h * HG_DK, (h + 1) * HG_DK)
        z = _dot(xb, w_ref[:, h * 4 * HG_DK:(h + 1) * 4 * HG_DK])
        zq, zf, zi, zg = (z[:, i * HG_DK:(i + 1) * HG_DK] for i in range(4))
        lb = lb_all[:, hs]
        gate = lb + (1.0 - lb) * _sigmoid(zf)
        lg = jnp.log2(gate)
        v_ref[:, hs] = zi.astype(v_ref.dtype)
        base = h * HG_WIDE
        slabs = {"q": zq * _sigmoid(zq), "k": 1.0 - gate, "gate": gn_ref[:, hs] * (zg * _sigmoid(zg)), "lg2": lg}
        for name, val in slabs.items():
            sl = HG_SLAB[name]
            o_ref[:, base + sl.start:base + sl.stop] = val
        risk = None
        for r0 in range(0, tm, half):
            s = -jnp.sum(lg[r0:r0 + half], axis=0, keepdims=True)
            risk = s if risk is None else jnp.maximum(risk, s)
        risk_ref[h:h + 1, :] = jnp.broadcast_to(jnp.max(risk, axis=1, keepdims=True), (1, LANES))


def _odd_front(x2, w, lb, gn):
    t = x2.shape[0]
    tm = TOKEN_TILE
    return pl.pallas_call(
        _odd_front_kernel,
        grid=(t // tm,),
        in_specs=[pl.BlockSpec((tm, x2.shape[1]), lambda i: (i, 0)), _resident(w.shape), _resident(lb.shape),
                  _resident(gn.shape)],
        out_specs=(pl.BlockSpec((tm, HG_HEADS * HG_WIDE), lambda i: (i, 0)),
                   pl.BlockSpec((tm, HG_HEADS * HG_DK), lambda i: (i, 0)),
                   pl.BlockSpec((None, HG_HEADS, LANES), lambda i: (i, 0, 0))),
        out_shape=(jax.ShapeDtypeStruct((t, HG_HEADS * HG_WIDE), jnp.float32),
                   jax.ShapeDtypeStruct((t, HG_HEADS * HG_DK), jnp.bfloat16),
                   jax.ShapeDtypeStruct((t // tm, HG_HEADS, LANES), jnp.float32)),
        compiler_params=_params(("parallel",)),
        name="odd_front",
    )(x2, w, lb, gn)


def _midpoint_rows(b, half):
    c = b.shape[0]
    if half >= SUBLANES:
        parts = []
        for p in range(c // (2 * half)):
            r = p * 2 * half + half - 1
            parts.append(jnp.broadcast_to(b[r:r + 1, :], (2 * half, LANES)))
        return parts[0] if len(parts) == 1 else jnp.concatenate(parts, axis=0)
    b3 = b.reshape(c // SUBLANES, SUBLANES, LANES)
    sub = lax.broadcasted_iota(jnp.int32, b3.shape, 1)
    out = None
    for p in range(SUBLANES // (2 * half)):
        r = p * 2 * half + half - 1
        cand = jnp.broadcast_to(b3[:, r:r + 1, :], b3.shape)
        out = cand if out is None else jnp.where(sub >= p * 2 * half, cand, out)
    return out.reshape(c, LANES)


def _hgrn_kernel(safe_ref, z_ref, v_ref, tril_ref, o_ref, st_ref):
    c = HG_CHUNK
    mid = c // 2 - 1
    n_chunks = z_ref.shape[0] // c
    n_levels = int(math.log2(c))
    bi, h, li = pl.program_id(0), pl.program_id(1), pl.program_id(2)
    safe = safe_ref[bi * pl.num_programs(2) + li, h] != 0

    @pl.when(li == 0)
    def _():
        st_ref[...] = jnp.zeros(st_ref.shape, jnp.float32)

    tril = tril_ref[...]

    def slab(rs, name):
        return z_ref[rs, HG_SLAB[name]]

    def cumsum(rs):
        lg = slab(rs, "lg2")
        hi = lg.astype(jnp.bfloat16)
        lo = (lg - hi.astype(jnp.float32)).astype(jnp.bfloat16)
        return _dot(tril, hi) + _dot(tril, lo)

    def load_qk(rs):
        return slab(rs, "q"), slab(rs, "k")

    t_idx = lax.broadcasted_iota(jnp.int32, (c, c), 0)
    s_idx = lax.broadcasted_iota(jnp.int32, (c, c), 1)

    def state_update(rs, kd, b_last):
        upd = lax.dot_general(v_ref[rs, :], kd, (((0,), (0,)), ((), ())), preferred_element_type=jnp.float32)
        return jnp.exp2(b_last), upd

    def chunk_output(rs, qd, st, attn):
        o = _dot_nt(qd, st.astype(jnp.bfloat16))
        o = o + _dot(attn.astype(jnp.bfloat16), v_ref[rs, :])
        o = o * lax.rsqrt(jnp.mean(o * o, axis=-1, keepdims=True) + NORM_EPS)
        o_ref[rs, :] = (o * slab(rs, "gate")).astype(o_ref.dtype)

    @pl.when(safe)
    def _():
        causal = s_idx <= t_idx
        chunks = [slice(ci * c, (ci + 1) * c) for ci in range(n_chunks)]
        bs = [cumsum(rs) for rs in chunks]
        scores, decays, upds, qds = [], [], [], []
        for rs, b in zip(chunks, bs):
            q, k = load_qk(rs)
            b_mid, b_last = b[mid:mid + 1, :], b[c - 1:c, :]
            d = b - b_mid
            qe = q * jnp.exp2(d)
            ke = k * jnp.exp2(-d)
            scores.append(_dot_nt(qe.astype(jnp.bfloat16), ke.astype(jnp.bfloat16)))
            decay, upd = state_update(rs, (ke * jnp.exp2(b_last - b_mid)).astype(jnp.bfloat16), b_last)
            decays.append(decay)
            upds.append(upd)
            qds.append((qe * jnp.exp2(b_mid)).astype(jnp.bfloat16))
        states = [st_ref[...]]
        for ci in range(n_chunks):
            states.append(states[-1] * decays[ci] + upds[ci])
        st_ref[...] = states[-1]
        for ci, rs in enumerate(chunks):
            chunk_output(rs, qds[ci], states[ci], jnp.where(causal, scores[ci], 0.0))

    @pl.when(jnp.logical_not(safe))
    def _():
        x = jnp.bitwise_xor(t_idx, s_idx)
        level = jnp.zeros((c, c), jnp.int32)
        for bit in range(n_levels):
            level = level + (x >= (1 << bit)).astype(jnp.int32)
        level = jnp.where(s_idx <= t_idx, level, -1)
        row_id = lax.broadcasted_iota(jnp.int32, (c, LANES), 0)

        def chunk_body(ci, carry):
            rs = pl.ds(pl.multiple_of(ci * c, c), c)
            (q, k), b = load_qk(rs), cumsum(rs)
            b_last = b[c - 1:c, :]
            attn = jnp.where(level == 0, _dot_nt(q.astype(jnp.bfloat16), k.astype(jnp.bfloat16)), 0.0)
            for lv in range(1, n_levels + 1):
                half = 1 << (lv - 1)
                right = jnp.bitwise_and(row_id, half) != 0
                d = b - _midpoint_rows(b, half)
                decay = jnp.exp2(jnp.where(right, d, -d))
                zl = (jnp.where(right, q, k) * decay).astype(jnp.bfloat16)
                attn = jnp.where(level == lv, _dot_nt(zl, zl), attn)
            st = st_ref[...]
            decay, upd = state_update(rs, (k * jnp.exp2(b_last - b)).astype(jnp.bfloat16), b_last)
            chunk_output(rs, (q * jnp.exp2(b)).astype(jnp.bfloat16), st, attn)
            st_ref[...] = st * decay + upd
            return carry

        lax.fori_loop(0, n_chunks, chunk_body, 0)


def _hgrn(safe, z, v, tril):
    b, s, _ = z.shape
    blk = HG_BLOCK
    grid_spec = pltpu.PrefetchScalarGridSpec(
        num_scalar_prefetch=1,
        grid=(b, HG_HEADS, s // blk),
        in_specs=[
            pl.BlockSpec((None, blk, HG_WIDE), lambda bi, h, li, safe_ref: (bi, li, h)),
            pl.BlockSpec((None, blk, HG_DK), lambda bi, h, li, safe_ref: (bi, li, h)),
            pl.BlockSpec(tril.shape, lambda bi, h, li, safe_ref: (0, 0), pipeline_mode=pl.Buffered(1)),
        ],
        out_specs=pl.BlockSpec((None, blk, HG_DV), lambda bi, h, li, safe_ref: (bi, li, h)),
        scratch_shapes=[pltpu.VMEM((HG_DV, HG_DK), jnp.float32)],
    )
    return pl.pallas_call(
        _hgrn_kernel,
        grid_spec=grid_spec,
        out_shape=jax.ShapeDtypeStruct((b, s, HG_HEADS * HG_DV), jnp.bfloat16),
        compiler_params=_params(("parallel", "parallel", "arbitrary")),
        name="hgrn2",
    )(safe, z, v, tril)


def kernel(x, positions, w_in_e, mla_gq, mla_gkv, w_qb, w_kvb, sgu_ln_g, sgu_ln_b, sgu_w, sgu_b, w_out_e,
           w_in_o, hg_lb, hg_gnorm, w_out_o, ln1_g, ln1_b, w_ff1, w_ff2, ln2_g, ln2_b):
    bsz, seq, d = x.shape
    assert d == D_MODEL and hg_lb.shape[0] == DEPTH == 2
    assert seq % ATTN_TILE == 0 and seq % HG_BLOCK == 0 and (bsz * seq) % TOKEN_TILE == 0
    t = bsz * seq
    bf = jnp.bfloat16
    f32 = jnp.float32
    row = lambda a: a.reshape(1, -1).astype(f32)

    inv_freq = ROPE_BASE ** (-jnp.arange(ROPE_HALF, dtype=f32) / ROPE_HALF)
    freq_col = jnp.broadcast_to(inv_freq[:, None], (ROPE_HALF, LANES))

    we = w_in_e[0]
    s0, s1, s2, s3 = MLA_LORA, 2 * MLA_LORA, 2 * MLA_LORA + MLA_ROPE, 2 * MLA_LORA + MLA_ROPE + SGU_DIM
    kr_cols = jnp.pad(we[:, s1:s2], ((0, 0), (ROPE_LO, HEAD_PAD - ROPE_LO - MLA_ROPE)))
    w_in_p = jnp.concatenate([we[:, :s1], kr_cols, we[:, s2:s3], we[:, s3:]], axis=1).astype(bf)
    assert w_in_p.shape[1] == EVEN_COLS
    dq = MLA_NOPE + MLA_ROPE
    wq_p = jnp.pad(w_qb[0].reshape(MLA_LORA, MLA_HEADS, dq), ((0, 0), (0, 0), (0, HEAD_PAD - dq)))
    wqt_p = wq_p.reshape(MLA_LORA, MLA_HEADS * HEAD_PAD).T.astype(bf)
    wkv = w_kvb[0].reshape(MLA_LORA, MLA_HEADS, MLA_NOPE + MLA_V)
    wk_p = jnp.pad(wkv[:, :, :MLA_NOPE], ((0, 0), (0, 0), (0, HEAD_PAD - MLA_NOPE)))
    wk_p = wk_p.reshape(MLA_LORA, MLA_HEADS * HEAD_PAD).astype(bf)
    wv_p = jnp.pad(wkv[:, :, MLA_NOPE:], ((0, 0), (0, 0), (0, HEAD_PAD - MLA_V)))
    wvt_p = wv_p.reshape(MLA_LORA, MLA_HEADS * HEAD_PAD).T.astype(bf)
    one_col = jnp.tile(jnp.zeros((HEAD_PAD,), f32).at[MLA_V].set(1.0), MLA_HEADS)
    one_col = jnp.broadcast_to(one_col[:, None], (MLA_HEADS * HEAD_PAD, LANES))
    sgu_bias = jnp.repeat(sgu_b[0].T, SGU_GROUP_DIM, axis=1).astype(f32)

    x2 = x.reshape(t, D_MODEL)
    qt, k, vt, b_out = _even_front(x2, positions.reshape(1, t), freq_col, w_in_p, row(mla_gq[0]), row(mla_gkv[0]),
                                   wqt_p, wk_p, wvt_p, one_col, row(sgu_ln_g[0]), row(sgu_ln_b[0]),
                                   sgu_w[0].astype(f32), sgu_bias)
    a_out = _attention(qt, k.reshape(bsz, seq, -1), vt, bsz, seq)
    a_rows = MLA_HEADS * MLA_V
    w1_bf, w2_bf = w_ff1.astype(bf), w_ff2.astype(bf)
    h1 = _post(0, x2, [a_out.reshape(t, -1), b_out], [w_out_e[0][:a_rows].astype(bf), w_out_e[0][a_rows:].astype(bf)],
               row(ln1_g[0]), row(ln1_b[0]), w1_bf, w2_bf, row(ln2_g[0]), row(ln2_b[0]))

    wo = w_in_o[0].reshape(D_MODEL, 4, HG_HEADS, HG_DK).transpose(0, 2, 1, 3).reshape(D_MODEL, 4 * HG_HEADS * HG_DK)
    z, v, risk = _odd_front(h1, wo.astype(bf), hg_lb.astype(f32), row(hg_gnorm[0]))
    risk = risk[:, :, 0].reshape(t // HG_BLOCK, HG_BLOCK // TOKEN_TILE, HG_HEADS).max(axis=1)
    safe = (risk <= HG_SAFE_EXPONENT * LOG2_E).astype(jnp.int32)
    tril = jnp.tril(jnp.ones((HG_CHUNK, HG_CHUNK), bf))
    o = _hgrn(safe, z.reshape(bsz, seq, -1), v.reshape(bsz, seq, -1), tril)
    h2 = _post(1, h1, [o.reshape(t, -1)], [w_out_o[0].astype(bf)], row(ln1_g[1]), row(ln1_b[1]), w1_bf, w2_bf,
               row(ln2_g[1]), row(ln2_b[1]))
    return h2.reshape(bsz, seq, D_MODEL)
```

```python
import functools
import math

import jax
import jax.numpy as jnp
from jax import lax
from jax.experimental import pallas as pl
from jax.experimental.pallas import tpu as pltpu

D_MODEL = 1024
DEPTH = 2
MLA_HEADS = 8
MLA_LORA = 256
MLA_NOPE = 64
MLA_ROPE = 32
MLA_V = 64
MLA_SCALE = (MLA_NOPE + MLA_ROPE) ** -0.5
LOG2_E = math.log2(math.e)
ROPE_BASE = 10000.0
SGU_GROUPS = 4
SGU_GROUP_DIM = 128
SGU_DIM = SGU_GROUPS * SGU_GROUP_DIM
SGU_CHUNK = 128
HG_HEADS = 8
HG_DK = 128
HG_DV = 128
D_FF = 4 * D_MODEL
DN_ALPHA = (2 * DEPTH) ** 0.25
NORM_EPS = 1e-5

LANES = 128
SUBLANES = 8
VMEM_LIMIT_BYTES = 56 * 1024 * 1024

TOKEN_TILE = 512
POST_TILE = 1024
POST_CHAIN_ROWS = 256
ATTN_TILE = 512
HG_CHUNK = 128
HG_BLOCK = 2048
FF_CHUNK = 1024
MASK_VALUE = -1e30
HG_SAFE_EXPONENT = 60.0

HEAD_PAD = LANES
COL_CQ = 0
COL_CKV = COL_CQ + MLA_LORA
COL_KR = COL_CKV + MLA_LORA
COL_U = COL_KR + HEAD_PAD
COL_V = COL_U + SGU_DIM
EVEN_COLS = COL_V + SGU_DIM
ROPE_HALF = MLA_ROPE // 2
ROPE_LO = MLA_NOPE


def _params(semantics):
    return pltpu.CompilerParams(dimension_semantics=semantics, vmem_limit_bytes=VMEM_LIMIT_BYTES)


def _resident(shape):
    zeros = (0,) * len(shape)
    return pl.BlockSpec(shape, lambda *_: zeros, pipeline_mode=pl.Buffered(1))


def _dot(a, b):
    return jnp.dot(a, b, preferred_element_type=jnp.float32)


def _dot_nt(a, b):
    return lax.dot_general(a, b, (((1,), (1,)), ((), ())), preferred_element_type=jnp.float32)


def _sigmoid(x):
    return 0.5 * jnp.tanh(0.5 * x) + 0.5


def _gelu_tanh(x):
    c = math.sqrt(2.0 / math.pi)
    return 0.5 * x * (1.0 + jnp.tanh(c * (x + 0.044715 * (x * x * x))))


def _layer_norm(x, g, b):
    mu = jnp.mean(x, axis=-1, keepdims=True)
    xc = x - mu
    var = jnp.mean(xc * xc, axis=-1, keepdims=True)
    return xc * lax.rsqrt(var + NORM_EPS) * g + b


def _rms_norm(x, g):
    return x * lax.rsqrt(jnp.mean(x * x, axis=-1, keepdims=True) + NORM_EPS) * g


def _even_front_kernel(x_ref, pos_ref, freq_ref, w_in_ref, gq_ref, gkv_ref, wqt_ref, wk_ref, wvt_ref, one_ref,
                       lng_ref, lnb_ref, sw_ref, sb_ref, qt_ref, k_ref, vt_ref, b_ref):
    tm = x_ref.shape[0]
    rep = tm // LANES
    xb = x_ref[...].astype(jnp.bfloat16)
    z = _dot(xb, w_in_ref[...])

    ang = jnp.tile(freq_ref[...], (1, rep)) * pos_ref[...].astype(jnp.float32)
    cos_t = jnp.cos(ang)
    sin_t = jnp.sin(ang)

    cq = _rms_norm(z[:, COL_CQ:COL_CQ + MLA_LORA], gq_ref[...] * (MLA_SCALE * LOG2_E)).astype(jnp.bfloat16)
    ckv = _rms_norm(z[:, COL_CKV:COL_CKV + MLA_LORA], gkv_ref[...]).astype(jnp.bfloat16)

    qt = _dot_nt(wqt_ref[...], cq)
    for h in range(MLA_HEADS):
        base = h * HEAD_PAD
        x1 = qt[base + ROPE_LO:base + ROPE_LO + ROPE_HALF]
        x2 = qt[base + ROPE_LO + ROPE_HALF:base + ROPE_LO + MLA_ROPE]
        slab = jnp.concatenate([qt[base:base + ROPE_LO], x1 * cos_t - x2 * sin_t, x2 * cos_t + x1 * sin_t,
                                qt[base + ROPE_LO + MLA_ROPE:base + HEAD_PAD]], axis=0)
        qt_ref[base:base + HEAD_PAD, :] = slab.astype(qt_ref.dtype)

    vt = _dot_nt(wvt_ref[...], ckv) + jnp.tile(one_ref[...], (1, rep))
    vt_ref[...] = vt.astype(vt_ref.dtype)

    ones = jnp.ones((ROPE_LO, tm), jnp.float32)
    zeros = jnp.zeros((HEAD_PAD - ROPE_LO - MLA_ROPE, tm), jnp.float32)
    cos_k = jnp.concatenate([ones, cos_t, cos_t, zeros], axis=0).T
    sin_k = jnp.concatenate([0.0 * ones, -sin_t, sin_t, zeros], axis=0).T
    lane = lax.broadcasted_iota(jnp.int32, (tm, LANES), 1)
    kr = z[:, COL_KR:COL_KR + HEAD_PAD]
    partner = jnp.where(lane < ROPE_LO + ROPE_HALF, pltpu.roll(kr, LANES - ROPE_HALF, 1), pltpu.roll(kr, ROPE_HALF, 1))
    kr = kr * cos_k + partner * sin_k
    kn = _dot(ckv, wk_ref[...])
    for h in range(MLA_HEADS):
        sl = slice(h * HEAD_PAD, (h + 1) * HEAD_PAD)
        k_ref[:, sl] = (kn[:, sl] + kr).astype(k_ref.dtype)

    u = _gelu_tanh(z[:, COL_U:COL_U + SGU_DIM])
    vn = _layer_norm(_gelu_tanh(z[:, COL_V:COL_V + SGU_DIM]), lng_ref[...], lnb_ref[...]).astype(jnp.bfloat16)
    row = lax.broadcasted_iota(jnp.int32, (SGU_CHUNK, SGU_CHUNK), 0)
    col = lax.broadcasted_iota(jnp.int32, (SGU_CHUNK, SGU_CHUNK), 1)
    causal = col <= row
    bias = sb_ref[...]
    for g in range(SGU_GROUPS):
        wg = jnp.where(causal, sw_ref[g], 0.0).astype(jnp.bfloat16)
        gs = slice(g * SGU_GROUP_DIM, (g + 1) * SGU_GROUP_DIM)
        for c in range(tm // SGU_CHUNK):
            rs = slice(c * SGU_CHUNK, (c + 1) * SGU_CHUNK)
            mixed = _dot(wg, vn[rs, gs]) + bias[:, gs]
            b_ref[rs, gs] = (u[rs, gs] * mixed).astype(b_ref.dtype)


def _even_front(x2, pos_row, freq_col, w_in, gq, gkv, wqt, wk, wvt, one_col, lng, lnb, sw, sb):
    t = x2.shape[0]
    tm = TOKEN_TILE
    row = lambda i: (i, 0)
    col = lambda i: (0, i)
    wide = MLA_HEADS * HEAD_PAD
    out_shapes = (
        jax.ShapeDtypeStruct((wide, t), jnp.bfloat16),
        jax.ShapeDtypeStruct((t, wide), jnp.bfloat16),
        jax.ShapeDtypeStruct((wide, t), jnp.bfloat16),
        jax.ShapeDtypeStruct((t, SGU_DIM), jnp.bfloat16),
    )
    consts = (freq_col, w_in, gq, gkv, wqt, wk, wvt, one_col, lng, lnb, sw, sb)
    return pl.pallas_call(
        _even_front_kernel,
        grid=(t // tm,),
        in_specs=[pl.BlockSpec((tm, D_MODEL), row), pl.BlockSpec((1, tm), col)] + [_resident(a.shape) for a in consts],
        out_specs=(
            pl.BlockSpec((wide, tm), col),
            pl.BlockSpec((tm, wide), row),
            pl.BlockSpec((wide, tm), col),
            pl.BlockSpec((tm, SGU_DIM), row),
        ),
        out_shape=out_shapes,
        compiler_params=_params(("parallel",)),
        name="even_front",
    )(x2, pos_row, *consts)


def _attn_kernel(step_ref, qt_ref, k_ref, vt_ref, o_ref, s_sc, cm_sc, m_sc, acc_sc):
    t = ATTN_TILE
    nq = qt_ref.shape[1] // t
    n_off = nq * (nq - 1) // 2
    assert n_off % 2 == 0 and n_off >= 2
    m_sc[...] = jnp.full(m_sc.shape, MASK_VALUE, jnp.float32)
    acc_sc[...] = jnp.zeros(acc_sc.shape, jnp.float32)

    def tile(i):
        return slice(i * t, (i + 1) * t) if isinstance(i, int) else pl.ds(pl.multiple_of(i * t, t), t)

    def scores(qi, j, slot, hh):
        hs = slice(hh * HEAD_PAD, (hh + 1) * HEAD_PAD)
        s = _dot(k_ref[tile(j), hs], qt_ref[hs, tile(qi)])
        s_sc[slot, hh] = s
        cm_sc[slot, hh] = jnp.broadcast_to(jnp.max(s, axis=0, keepdims=True), (SUBLANES, t))

    def update(qi, j, slot, hh):
        hs = slice(hh * HEAD_PAD, (hh + 1) * HEAD_PAD)
        m_prev = m_sc[qi, hh]
        m_next = jnp.maximum(m_prev, cm_sc[slot, hh])
        p = jnp.exp2(s_sc[slot, hh] - m_next[0:1, :]).astype(jnp.bfloat16)
        alpha = jnp.exp2(m_prev - m_next)
        acc_sc[qi, hh] = alpha[0:1, :] * acc_sc[qi, hh] + _dot(vt_ref[hs, tile(j)], p)
        m_sc[qi, hh] = m_next

    half = t // 2
    key = lax.broadcasted_iota(jnp.int32, (half, half), 0)
    qry = lax.broadcasted_iota(jnp.int32, (half, half), 1)
    causal = key <= qry

    def scores_diag(qi, slot, hh):
        hs = slice(hh * HEAD_PAD, (hh + 1) * HEAD_PAD)
        lo, mid_, hi = qi * t, qi * t + half, (qi + 1) * t
        top = _dot(k_ref[lo:mid_, hs], qt_ref[hs, lo:hi])
        bot = jnp.where(causal, _dot(k_ref[mid_:hi, hs], qt_ref[hs, mid_:hi]), MASK_VALUE)
        top_left = jnp.where(causal, top[:, :half], MASK_VALUE)
        s_sc[slot, hh, 0:half, 0:half] = top_left
        s_sc[slot, hh, 0:half, half:t] = top[:, half:]
        s_sc[slot, hh, half:t, half:t] = bot
        cm = jnp.concatenate([jnp.max(top_left, axis=0, keepdims=True),
                              jnp.maximum(jnp.max(top[:, half:], axis=0, keepdims=True),
                                          jnp.max(bot, axis=0, keepdims=True))], axis=1)
        cm_sc[slot, hh] = jnp.broadcast_to(cm, (SUBLANES, t))

    def update_diag(qi, slot, hh):
        hs = slice(hh * HEAD_PAD, (hh + 1) * HEAD_PAD)
        lo, mid_, hi = qi * t, qi * t + half, (qi + 1) * t
        m_prev = m_sc[qi, hh]
        m_next = jnp.maximum(m_prev, cm_sc[slot, hh])
        p_top = jnp.exp2(s_sc[slot, hh, 0:half, :] - m_next[0:1, :]).astype(jnp.bfloat16)
        p_bot = jnp.exp2(s_sc[slot, hh, half:t, half:t] - m_next[0:1, half:]).astype(jnp.bfloat16)
        pv = _dot(vt_ref[hs, lo:mid_], p_top)
        pv = jnp.concatenate([pv[:, :half], pv[:, half:] + _dot(vt_ref[hs, mid_:hi], p_bot)], axis=1)
        return jnp.exp2(m_prev - m_next)[0:1, :] * acc_sc[qi, hh] + pv

    def off_step(n):
        return step_ref[0, n], step_ref[1, n]

    for hh in range(2):
        scores(*off_step(0), 0, hh)

    def body(p, carry):
        for slot in range(2):
            n = 2 * p + slot
            for hh in range(2):
                scores(*off_step(n + 1), 1 - slot, hh)
                update(*off_step(n), slot, hh)
        return carry

    lax.fori_loop(0, n_off // 2 - 1, body, 0)
    for hh in range(2):
        scores(*off_step(n_off - 1), 1, hh)
        update(*off_step(n_off - 2), 0, hh)
    for hh in range(2):
        scores_diag(0, 0, hh)
        update(*off_step(n_off - 1), 1, hh)

    lane = lax.broadcasted_iota(jnp.int32, (t, LANES), 1)
    for qi in range(nq):
        slot = qi % 2
        outs = []
        for hh in range(2):
            if qi + 1 < nq:
                scores_diag(qi + 1, 1 - slot, hh)
            acc = update_diag(qi, slot, hh)
            outs.append((acc / acc[MLA_V:MLA_V + 1, :]).T)
        pair = jnp.where(lane < MLA_V, outs[0], pltpu.roll(outs[1], MLA_V, 1))
        o_ref[qi * t:(qi + 1) * t, :] = pair.astype(o_ref.dtype)


def _attention(qt, k, vt, bsz, seq):
    t = ATTN_TILE
    nq = seq // t
    pairs = MLA_HEADS // 2
    steps = jnp.asarray([[qi for qi in range(nq) for _ in range(qi)],
                         [j for qi in range(nq) for j in range(qi)]], jnp.int32)
    grid_spec = pltpu.PrefetchScalarGridSpec(
        num_scalar_prefetch=1,
        grid=(bsz, pairs),
        in_specs=[
            pl.BlockSpec((2 * HEAD_PAD, seq), lambda bi, p, steps_ref: (p, bi)),
            pl.BlockSpec((None, seq, 2 * HEAD_PAD), lambda bi, p, steps_ref: (bi, 0, p)),
            pl.BlockSpec((2 * HEAD_PAD, seq), lambda bi, p, steps_ref: (p, bi)),
        ],
        out_specs=pl.BlockSpec((None, seq, 2 * MLA_V), lambda bi, p, steps_ref: (bi, 0, p)),
        scratch_shapes=[
            pltpu.VMEM((2, 2, t, t), jnp.float32),
            pltpu.VMEM((2, 2, SUBLANES, t), jnp.float32),
            pltpu.VMEM((nq, 2, SUBLANES, t), jnp.float32),
            pltpu.VMEM((nq, 2, HEAD_PAD, t), jnp.float32),
        ],
    )
    return pl.pallas_call(
        _attn_kernel,
        grid_spec=grid_spec,
        out_shape=jax.ShapeDtypeStruct((bsz, seq, MLA_HEADS * MLA_V), jnp.bfloat16),
        compiler_params=_params(("parallel", "parallel")),
        name="mla_attention",
    )(steps, qt, k, vt)


def _post_kernel(n_mix, *refs):
    h_ref = refs[0]
    mix_refs = refs[1:1 + n_mix]
    wout_refs = refs[1 + n_mix:1 + 2 * n_mix]
    g1_ref, b1_ref, w1_ref, w2_ref, g2_ref, b2_ref, o_ref = refs[1 + 2 * n_mix:]
    tm = h_ref.shape[0]
    halves = [slice(r, r + POST_CHAIN_ROWS) for r in range(0, tm, POST_CHAIN_ROWS)]
    mixes = []
    for rows in halves:
        mix = _dot(mix_refs[0][rows, :], wout_refs[0][...])
        for m_ref, w_ref in zip(mix_refs[1:], wout_refs[1:]):
            mix = mix + _dot(m_ref[rows, :], w_ref[...])
        mixes.append(mix)
    for rows, mix in zip(halves, mixes):
        y = _layer_norm(DN_ALPHA * h_ref[rows, :] + mix, g1_ref[...], b1_ref[...])
        yb = y.astype(jnp.bfloat16)
        ff = None
        for c in range(D_FF // FF_CHUNK):
            cs = slice(c * FF_CHUNK, (c + 1) * FF_CHUNK)
            a = jnp.maximum(_dot(yb, w1_ref[:, cs]), 0.0)
            part = _dot((a * a).astype(jnp.bfloat16), w2_ref[cs, :])
            ff = part if ff is None else ff + part
        o_ref[rows, :] = _layer_norm(DN_ALPHA * y + ff, g2_ref[...], b2_ref[...])


def _post(layer, h2, mixes, wouts, g1, b1, w1, w2, g2, b2):
    t = h2.shape[0]
    tm = POST_TILE
    row = lambda i: (i, 0)
    n = len(mixes)
    layer_slab = lambda a: pl.BlockSpec((None,) + a.shape[1:], lambda i: (layer, 0, 0), pipeline_mode=pl.Buffered(1))
    in_specs = [pl.BlockSpec((tm, D_MODEL), row)]
    in_specs += [pl.BlockSpec((tm, m.shape[1]), row) for m in mixes]
    in_specs += [_resident(w.shape) for w in wouts]
    in_specs += [_resident(g1.shape), _resident(b1.shape), layer_slab(w1), layer_slab(w2), _resident(g2.shape),
                 _resident(b2.shape)]
    return pl.pallas_call(
        functools.partial(_post_kernel, n),
        grid=(t // tm,),
        in_specs=in_specs,
        out_specs=pl.BlockSpec((tm, D_MODEL), row),
        out_shape=jax.ShapeDtypeStruct((t, D_MODEL), jnp.float32),
        compiler_params=_params(("parallel",)),
        name="post",
    )(h2, *mixes, *wouts, g1, b1, w1, w2, g2, b2)


HG_SLABS = ("q", "k", "v", "gate")
HG_SLAB = {name: slice(i * HG_DK, (i + 1) * HG_DK) for i, name in enumerate(HG_SLABS)}
HG_WIDE = len(HG_SLABS) * HG_DK


def _odd_front_kernel(x_ref, w_ref, lb_ref, gn_ref, o_ref, lg_ref, risk_ref):
    tm = x_ref.shape[0]
    half = HG_CHUNK // 2
    xb = x_ref[...].astype(jnp.bfloat16)
    lbp = lb_ref[...]
    mx = jnp.max(lbp, axis=0, keepdims=True)
    e = jnp.exp(lbp - mx)
    sm = e / jnp.sum(e, axis=0, keepdims=True)
    lb_all = (sm[0:1, :] + sm[1:2, :]) - sm[0:1, :]
    for h in range(HG_HEADS):
        hs = slice(h * HG_DK, (h + 1) * HG_DK)
        z = _dot(xb, w_ref[:, h * 4 * HG_DK:(h + 1) * 4 * HG_DK])
        zq, zf, zi, zg = (z[:, i * HG_DK:(i + 1) * HG_DK] for i in range(4))
        lb = lb_all[:, hs]
        gate = lb + (1.0 - lb) * _sigmoid(zf)
        lg = jnp.log2(gate)
        lg_ref[:, hs] = lg
        base = h * HG_WIDE
        slabs = {"q": zq * _sigmoid(zq), "k": 1.0 - gate, "v": zi, "gate": gn_ref[:, hs] * (zg * _sigmoid(zg))}
        for name, val in slabs.items():
            sl = HG_SLAB[name]
            o_ref[:, base + sl.start:base + sl.stop] = val.astype(o_ref.dtype)
        risk = None
        for r0 in range(0, tm, half):
            s = -jnp.sum(lg[r0:r0 + half], axis=0, keepdims=True)
            risk = s if risk is None else jnp.maximum(risk, s)
        risk_ref[h:h + 1, :] = jnp.broadcast_to(jnp.max(risk, axis=1, keepdims=True), (1, LANES))


def _odd_front(x2, w, lb, gn):
    t = x2.shape[0]
    tm = TOKEN_TILE
    return pl.pallas_call(
        _odd_front_kernel,
        grid=(t // tm,),
        in_specs=[pl.BlockSpec((tm, x2.shape[1]), lambda i: (i, 0)), _resident(w.shape), _resident(lb.shape),
                  _resident(gn.shape)],
        out_specs=(pl.BlockSpec((tm, HG_HEADS * HG_WIDE), lambda i: (i, 0)),
                   pl.BlockSpec((tm, HG_HEADS * HG_DK), lambda i: (i, 0)),
                   pl.BlockSpec((None, HG_HEADS, LANES), lambda i: (i, 0, 0))),
        out_shape=(jax.ShapeDtypeStruct((t, HG_HEADS * HG_WIDE), jnp.bfloat16),
                   jax.ShapeDtypeStruct((t, HG_HEADS * HG_DK), jnp.float32),
                   jax.ShapeDtypeStruct((t // tm, HG_HEADS, LANES), jnp.float32)),
        compiler_params=_params(("parallel",)),
        name="odd_front",
    )(x2, w, lb, gn)


def _midpoint_rows(b, half):
    c = b.shape[0]
    if half >= SUBLANES:
        parts = []
        for p in range(c // (2 * half)):
            r = p * 2 * half + half - 1
            parts.append(jnp.broadcast_to(b[r:r + 1, :], (2 * half, LANES)))
        return parts[0] if len(parts) == 1 else jnp.concatenate(parts, axis=0)
    b3 = b.reshape(c // SUBLANES, SUBLANES, LANES)
    sub = lax.broadcasted_iota(jnp.int32, b3.shape, 1)
    out = None
    for p in range(SUBLANES // (2 * half)):
        r = p * 2 * half + half - 1
        cand = jnp.broadcast_to(b3[:, r:r + 1, :], b3.shape)
        out = cand if out is None else jnp.where(sub >= p * 2 * half, cand, out)
    return out.reshape(c, LANES)


def _hgrn_kernel(safe_ref, z_ref, lg_ref, tril_ref, o_ref, st_ref):
    c = HG_CHUNK
    mid = c // 2 - 1
    n_chunks = z_ref.shape[0] // c
    n_levels = int(math.log2(c))
    bi, h, li = pl.program_id(0), pl.program_id(1), pl.program_id(2)
    safe = safe_ref[bi * pl.num_programs(2) + li, h] != 0

    @pl.when(li == 0)
    def _():
        st_ref[...] = jnp.zeros(st_ref.shape, jnp.float32)

    tril = tril_ref[...]

    def slab(rs, name):
        return z_ref[rs, HG_SLAB[name]]

    def cumsum(rs):
        lg = lg_ref[rs, :]
        hi = lg.astype(jnp.bfloat16)
        lo = (lg - hi.astype(jnp.float32)).astype(jnp.bfloat16)
        return _dot(tril, hi) + _dot(tril, lo)

    def load_qk(rs):
        return slab(rs, "q").astype(jnp.float32), slab(rs, "k").astype(jnp.float32)

    t_idx = lax.broadcasted_iota(jnp.int32, (c, c), 0)
    s_idx = lax.broadcasted_iota(jnp.int32, (c, c), 1)

    def state_update(rs, kd, b_last):
        upd = lax.dot_general(slab(rs, "v"), kd, (((0,), (0,)), ((), ())), preferred_element_type=jnp.float32)
        return jnp.exp2(b_last), upd

    def chunk_output(rs, qd, st, attn):
        o = _dot_nt(qd, st.astype(jnp.bfloat16))
        o = o + _dot(attn.astype(jnp.bfloat16), slab(rs, "v"))
        o = o * lax.rsqrt(jnp.mean(o * o, axis=-1, keepdims=True) + NORM_EPS)
        o_ref[rs, :] = (o * slab(rs, "gate").astype(jnp.float32)).astype(o_ref.dtype)

    @pl.when(safe)
    def _():
        causal = s_idx <= t_idx
        chunks = [slice(ci * c, (ci + 1) * c) for ci in range(n_chunks)]
        bs = [cumsum(rs) for rs in chunks]
        scores, decays, upds, qds = [], [], [], []
        for rs, b in zip(chunks, bs):
            q, k = load_qk(rs)
            b_mid, b_last = b[mid:mid + 1, :], b[c - 1:c, :]
            d = b - b_mid
            qe = q * jnp.exp2(d)
            ke = k * jnp.exp2(-d)
            scores.append(_dot_nt(qe.astype(jnp.bfloat16), ke.astype(jnp.bfloat16)))
            decay, upd = state_update(rs, (ke * jnp.exp2(b_last - b_mid)).astype(jnp.bfloat16), b_last)
            decays.append(decay)
            upds.append(upd)
            qds.append((qe * jnp.exp2(b_mid)).astype(jnp.bfloat16))
        states = [st_ref[...]]
        for ci in range(n_chunks):
            states.append(states[-1] * decays[ci] + upds[ci])
        st_ref[...] = states[-1]
        for ci, rs in enumerate(chunks):
            chunk_output(rs, qds[ci], states[ci], jnp.where(causal, scores[ci], 0.0))

    @pl.when(jnp.logical_not(safe))
    def _():
        x = jnp.bitwise_xor(t_idx, s_idx)
        level = jnp.zeros((c, c), jnp.int32)
        for bit in range(n_levels):
            level = level + (x >= (1 << bit)).astype(jnp.int32)
        level = jnp.where(s_idx <= t_idx, level, -1)
        row_id = lax.broadcasted_iota(jnp.int32, (c, LANES), 0)

        def chunk_body(ci, carry):
            rs = pl.ds(pl.multiple_of(ci * c, c), c)
            (q, k), b = load_qk(rs), cumsum(rs)
            b_last = b[c - 1:c, :]
            attn = jnp.where(level == 0, _dot_nt(q.astype(jnp.bfloat16), k.astype(jnp.bfloat16)), 0.0)
            for lv in range(1, n_levels + 1):
                half = 1 << (lv - 1)
                right = jnp.bitwise_and(row_id, half) != 0
                d = b - _midpoint_rows(b, half)
                decay = jnp.exp2(jnp.where(right, d, -d))
                zl = (jnp.where(right, q, k) * decay).astype(jnp.bfloat16)
                attn = jnp.where(level == lv, _dot_nt(zl, zl), attn)
            st = st_ref[...]
            decay, upd = state_update(rs, (k * jnp.exp2(b_last - b)).astype(jnp.bfloat16), b_last)
            chunk_output(rs, (q * jnp.exp2(b)).astype(jnp.bfloat16), st, attn)
            st_ref[...] = st * decay + upd
            return carry

        lax.fori_loop(0, n_chunks, chunk_body, 0)


def _hgrn(safe, z, lg, tril):
    b, s, _ = z.shape
    blk = HG_BLOCK
    grid_spec = pltpu.PrefetchScalarGridSpec(
        num_scalar_prefetch=1,
        grid=(b, HG_HEADS, s // blk),
        in_specs=[
            pl.BlockSpec((None, blk, HG_WIDE), lambda bi, h, li, safe_ref: (bi, li, h)),
            pl.BlockSpec((None, blk, HG_DK), lambda bi, h, li, safe_ref: (bi, li, h)),
            pl.BlockSpec(tril.shape, lambda bi, h, li, safe_ref: (0, 0), pipeline_mode=pl.Buffered(1)),
        ],
        out_specs=pl.BlockSpec((None, blk, HG_DV), lambda bi, h, li, safe_ref: (bi, li, h)),
        scratch_shapes=[pltpu.VMEM((HG_DV, HG_DK), jnp.float32)],
    )
    return pl.pallas_call(
        _hgrn_kernel,
        grid_spec=grid_spec,
        out_shape=jax.ShapeDtypeStruct((b, s, HG_HEADS * HG_DV), jnp.bfloat16),
        compiler_params=_params(("parallel", "parallel", "arbitrary")),
        name="hgrn2",
    )(safe, z, lg, tril)


def kernel(x, positions, w_in_e, mla_gq, mla_gkv, w_qb, w_kvb, sgu_ln_g, sgu_ln_b, sgu_w, sgu_b, w_out_e,
           w_in_o, hg_lb, hg_gnorm, w_out_o, ln1_g, ln1_b, w_ff1, w_ff2, ln2_g, ln2_b):
    bsz, seq, d = x.shape
    assert d == D_MODEL and hg_lb.shape[0] == DEPTH == 2
    assert seq % ATTN_TILE == 0 and seq % HG_BLOCK == 0 and (bsz * seq) % TOKEN_TILE == 0
    t = bsz * seq
    bf = jnp.bfloat16
    f32 = jnp.float32
    row = lambda a: a.reshape(1, -1).astype(f32)

    inv_freq = ROPE_BASE ** (-jnp.arange(ROPE_HALF, dtype=f32) / ROPE_HALF)
    freq_col = jnp.broadcast_to(inv_freq[:, None], (ROPE_HALF, LANES))

    we = w_in_e[0].astype(bf)
    s0, s1, s2, s3 = MLA_LORA, 2 * MLA_LORA, 2 * MLA_LORA + MLA_ROPE, 2 * MLA_LORA + MLA_ROPE + SGU_DIM
    kr_cols = jnp.pad(we[:, s1:s2], ((0, 0), (ROPE_LO, HEAD_PAD - ROPE_LO - MLA_ROPE)))
    w_in_p = jnp.concatenate([we[:, :s1], kr_cols, we[:, s2:s3], we[:, s3:]], axis=1)
    assert w_in_p.shape[1] == EVEN_COLS
    dq = MLA_NOPE + MLA_ROPE
    wq_p = jnp.pad(w_qb[0].reshape(MLA_LORA, MLA_HEADS, dq), ((0, 0), (0, 0), (0, HEAD_PAD - dq)))
    wqt_p = wq_p.reshape(MLA_LORA, MLA_HEADS * HEAD_PAD).T.astype(bf)
    wkv = w_kvb[0].reshape(MLA_LORA, MLA_HEADS, MLA_NOPE + MLA_V)
    wk_p = jnp.pad(wkv[:, :, :MLA_NOPE], ((0, 0), (0, 0), (0, HEAD_PAD - MLA_NOPE)))
    wk_p = wk_p.reshape(MLA_LORA, MLA_HEADS * HEAD_PAD).astype(bf)
    wv_p = jnp.pad(wkv[:, :, MLA_NOPE:], ((0, 0), (0, 0), (0, HEAD_PAD - MLA_V)))
    wvt_p = wv_p.reshape(MLA_LORA, MLA_HEADS * HEAD_PAD).T.astype(bf)
    one_col = jnp.tile(jnp.zeros((HEAD_PAD,), f32).at[MLA_V].set(1.0), MLA_HEADS)
    one_col = jnp.broadcast_to(one_col[:, None], (MLA_HEADS * HEAD_PAD, LANES))
    sgu_bias = jnp.repeat(sgu_b[0].T, SGU_GROUP_DIM, axis=1).astype(f32)

    x2 = x.reshape(t, D_MODEL)
    qt, k, vt, b_out = _even_front(x2, positions.reshape(1, t), freq_col, w_in_p, row(mla_gq[0]), row(mla_gkv[0]),
                                   wqt_p, wk_p, wvt_p, one_col, row(sgu_ln_g[0]), row(sgu_ln_b[0]),
                                   sgu_w[0].astype(f32), sgu_bias)
    a_out = _attention(qt, k.reshape(bsz, seq, -1), vt, bsz, seq)
    a_rows = MLA_HEADS * MLA_V
    w1_bf, w2_bf = w_ff1.astype(bf), w_ff2.astype(bf)
    h1 = _post(0, x2, [a_out.reshape(t, -1), b_out], [w_out_e[0][:a_rows].astype(bf), w_out_e[0][a_rows:].astype(bf)],
               row(ln1_g[0]), row(ln1_b[0]), w1_bf, w2_bf, row(ln2_g[0]), row(ln2_b[0]))

    wo = w_in_o[0].astype(bf).reshape(D_MODEL, 4, HG_HEADS, HG_DK).transpose(0, 2, 1, 3)
    wo = wo.reshape(D_MODEL, 4 * HG_HEADS * HG_DK)
    z, lg, risk = _odd_front(h1, wo, hg_lb.astype(f32), row(hg_gnorm[0]))
    risk = risk[:, :, 0].reshape(t // HG_BLOCK, HG_BLOCK // TOKEN_TILE, HG_HEADS).max(axis=1)
    safe = (risk <= HG_SAFE_EXPONENT * LOG2_E).astype(jnp.int32)
    tril = jnp.tril(jnp.ones((HG_CHUNK, HG_CHUNK), bf))
    o = _hgrn(safe, z.reshape(bsz, seq, -1), lg.reshape(bsz, seq, -1), tril)
    h2 = _post(1, h1, [o.reshape(t, -1)], [w_out_o[0].astype(bf)], row(ln1_g[1]), row(ln1_b[1]), w1_bf, w2_bf,
               row(ln2_g[1]), row(ln2_b[1]))
    return h2.reshape(bsz, seq, D_MODEL)
```

```python
import functools
import math

import jax
import jax.numpy as jnp
from jax import lax
from jax.experimental import pallas as pl
from jax.experimental.pallas import tpu as pltpu

D_MODEL = 1024
DEPTH = 2
MLA_HEADS = 8
MLA_LORA = 256
MLA_NOPE = 64
MLA_ROPE = 32
MLA_V = 64
MLA_SCALE = (MLA_NOPE + MLA_ROPE) ** -0.5
LOG2_E = math.log2(math.e)
ROPE_BASE = 10000.0
SGU_GROUPS = 4
SGU_GROUP_DIM = 128
SGU_DIM = SGU_GROUPS * SGU_GROUP_DIM
SGU_CHUNK = 128
HG_HEADS = 8
HG_DK = 128
HG_DV = 128
D_FF = 4 * D_MODEL
DN_ALPHA = (2 * DEPTH) ** 0.25
NORM_EPS = 1e-5

LANES = 128
SUBLANES = 8
VMEM_LIMIT_BYTES = 56 * 1024 * 1024

TOKEN_TILE = 1024
EVEN_TILE = 1024
FRONT_CHAIN_ROWS = 256
POST_TILE = 1024
POST_CHAIN_ROWS = 256
ATTN_TILE = 512
HG_CHUNK = 128
HG_BLOCK = 2048
FF_CHUNK = 1024
MASK_VALUE = -1e30
HG_SAFE_EXPONENT = 60.0

HEAD_PAD = LANES
COL_CQ = 0
COL_CKV = COL_CQ + MLA_LORA
COL_KR = COL_CKV + MLA_LORA
COL_U = COL_KR + HEAD_PAD
COL_V = COL_U + SGU_DIM
EVEN_COLS = COL_V + SGU_DIM
ROPE_HALF = MLA_ROPE // 2
ROPE_LO = MLA_NOPE


def _params(semantics):
    return pltpu.CompilerParams(dimension_semantics=semantics, vmem_limit_bytes=VMEM_LIMIT_BYTES)


def _resident(shape):
    zeros = (0,) * len(shape)
    return pl.BlockSpec(shape, lambda *_: zeros, pipeline_mode=pl.Buffered(1))


def _dot(a, b):
    return jnp.dot(a, b, preferred_element_type=jnp.float32)


def _dot_nt(a, b):
    return lax.dot_general(a, b, (((1,), (1,)), ((), ())), preferred_element_type=jnp.float32)


def _gelu_tanh(x):
    c = math.sqrt(2.0 / math.pi)
    return 0.5 * x * (1.0 + jnp.tanh(c * (x + 0.044715 * (x * x * x))))


def _layer_norm(x, g, b):
    mu = jnp.mean(x, axis=-1, keepdims=True)
    xc = x - mu
    var = jnp.mean(xc * xc, axis=-1, keepdims=True)
    return xc * lax.rsqrt(var + NORM_EPS) * g + b


def _rms_norm(x, g):
    return x * lax.rsqrt(jnp.mean(x * x, axis=-1, keepdims=True) + NORM_EPS) * g


def _even_front_kernel(x_ref, pos_ref, freq_ref, w_in_ref, gq_ref, gkv_ref, wqt_ref, wk_ref, wvt_ref, one_ref,
                       lng_ref, lnb_ref, sw_ref, sb_ref, qt_ref, k_ref, vt_ref, b_ref):
    chains = [slice(r, r + FRONT_CHAIN_ROWS) for r in range(0, x_ref.shape[0], FRONT_CHAIN_ROWS)]
    zs = [_dot(x_ref[rows, :].astype(jnp.bfloat16), w_in_ref[...]) for rows in chains]
    for rows, z in zip(chains, zs):
        _even_front_chain(rows, z, pos_ref, freq_ref, gq_ref, gkv_ref, wqt_ref, wk_ref, wvt_ref, one_ref,
                          lng_ref, lnb_ref, sw_ref, sb_ref, qt_ref, k_ref, vt_ref, b_ref)


def _even_front_chain(rows, z, pos_ref, freq_ref, gq_ref, gkv_ref, wqt_ref, wk_ref, wvt_ref, one_ref,
                      lng_ref, lnb_ref, sw_ref, sb_ref, qt_ref, k_ref, vt_ref, b_ref):
    tm = z.shape[0]
    rep = tm // LANES

    ang = jnp.tile(freq_ref[...], (1, rep)) * pos_ref[:, rows].astype(jnp.float32)
    cos_t = jnp.cos(ang)
    sin_t = jnp.sin(ang)

    cq = _rms_norm(z[:, COL_CQ:COL_CQ + MLA_LORA], gq_ref[...] * (MLA_SCALE * LOG2_E)).astype(jnp.bfloat16)
    ckv = _rms_norm(z[:, COL_CKV:COL_CKV + MLA_LORA], gkv_ref[...]).astype(jnp.bfloat16)

    qt = _dot_nt(wqt_ref[...], cq)
    for h in range(MLA_HEADS):
        base = h * HEAD_PAD
        x1 = qt[base + ROPE_LO:base + ROPE_LO + ROPE_HALF]
        x2 = qt[base + ROPE_LO + ROPE_HALF:base + ROPE_LO + MLA_ROPE]
        slab = jnp.concatenate([qt[base:base + ROPE_LO], x1 * cos_t - x2 * sin_t, x2 * cos_t + x1 * sin_t,
                                qt[base + ROPE_LO + MLA_ROPE:base + HEAD_PAD]], axis=0)
        qt_ref[base:base + HEAD_PAD, rows] = slab.astype(qt_ref.dtype)

    vt = _dot_nt(wvt_ref[...], ckv) + jnp.tile(one_ref[...], (1, rep))
    vt_ref[:, rows] = vt.astype(vt_ref.dtype)

    ones = jnp.ones((ROPE_LO, tm), jnp.float32)
    zeros = jnp.zeros((HEAD_PAD - ROPE_LO - MLA_ROPE, tm), jnp.float32)
    cos_k = jnp.concatenate([ones, cos_t, cos_t, zeros], axis=0).T
    sin_k = jnp.concatenate([0.0 * ones, -sin_t, sin_t, zeros], axis=0).T
    lane = lax.broadcasted_iota(jnp.int32, (tm, LANES), 1)
    kr = z[:, COL_KR:COL_KR + HEAD_PAD]
    partner = jnp.where(lane < ROPE_LO + ROPE_HALF, pltpu.roll(kr, LANES - ROPE_HALF, 1), pltpu.roll(kr, ROPE_HALF, 1))
    kr = kr * cos_k + partner * sin_k
    kn = _dot(ckv, wk_ref[...])
    for h in range(MLA_HEADS):
        sl = slice(h * HEAD_PAD, (h + 1) * HEAD_PAD)
        k_ref[rows, sl] = (kn[:, sl] + kr).astype(k_ref.dtype)

    u = _gelu_tanh(z[:, COL_U:COL_U + SGU_DIM])
    vn = _layer_norm(_gelu_tanh(z[:, COL_V:COL_V + SGU_DIM]), lng_ref[...], lnb_ref[...]).astype(jnp.bfloat16)
    row = lax.broadcasted_iota(jnp.int32, (SGU_CHUNK, SGU_CHUNK), 0)
    col = lax.broadcasted_iota(jnp.int32, (SGU_CHUNK, SGU_CHUNK), 1)
    causal = col <= row
    bias = sb_ref[...]
    for g in range(SGU_GROUPS):
        wg = jnp.where(causal, sw_ref[g], 0.0).astype(jnp.bfloat16)
        gs = slice(g * SGU_GROUP_DIM, (g + 1) * SGU_GROUP_DIM)
        for c in range(tm // SGU_CHUNK):
            rs = slice(c * SGU_CHUNK, (c + 1) * SGU_CHUNK)
            mixed = _dot(wg, vn[rs, gs]) + bias[:, gs]
            out_rows = slice(rows.start + rs.start, rows.start + rs.stop)
            b_ref[out_rows, gs] = (u[rs, gs] * mixed).astype(b_ref.dtype)


def _even_front(x2, pos_row, freq_col, w_in, gq, gkv, wqt, wk, wvt, one_col, lng, lnb, sw, sb):
    t = x2.shape[0]
    tm = EVEN_TILE
    row = lambda i: (i, 0)
    col = lambda i: (0, i)
    wide = MLA_HEADS * HEAD_PAD
    out_shapes = (
        jax.ShapeDtypeStruct((wide, t), jnp.bfloat16),
        jax.ShapeDtypeStruct((t, wide), jnp.bfloat16),
        jax.ShapeDtypeStruct((wide, t), jnp.bfloat16),
        jax.ShapeDtypeStruct((t, SGU_DIM), jnp.bfloat16),
    )
    consts = (freq_col, w_in, gq, gkv, wqt, wk, wvt, one_col, lng, lnb, sw, sb)
    return pl.pallas_call(
        _even_front_kernel,
        grid=(t // tm,),
        in_specs=[pl.BlockSpec((tm, D_MODEL), row), pl.BlockSpec((1, tm), col)] + [_resident(a.shape) for a in consts],
        out_specs=(
            pl.BlockSpec((wide, tm), col),
            pl.BlockSpec((tm, wide), row),
            pl.BlockSpec((wide, tm), col),
            pl.BlockSpec((tm, SGU_DIM), row),
        ),
        out_shape=out_shapes,
        compiler_params=_params(("parallel",)),
        name="even_front",
    )(x2, pos_row, *consts)


def _attn_kernel(step_ref, qt_ref, k_ref, vt_ref, o_ref, s_sc, cm_sc, m_sc, acc_sc):
    t = ATTN_TILE
    nq = qt_ref.shape[1] // t
    n_off = nq * (nq - 1) // 2
    assert n_off % 2 == 0 and n_off >= 2
    m_sc[...] = jnp.full(m_sc.shape, MASK_VALUE, jnp.float32)
    acc_sc[...] = jnp.zeros(acc_sc.shape, jnp.float32)

    def tile(i):
        return slice(i * t, (i + 1) * t) if isinstance(i, int) else pl.ds(pl.multiple_of(i * t, t), t)

    def scores(qi, j, slot, hh):
        hs = slice(hh * HEAD_PAD, (hh + 1) * HEAD_PAD)
        s = _dot(k_ref[tile(j), hs], qt_ref[hs, tile(qi)])
        s_sc[slot, hh] = s
        cm_sc[slot, hh] = jnp.broadcast_to(jnp.max(s, axis=0, keepdims=True), (SUBLANES, t))

    def update(qi, j, slot, hh):
        hs = slice(hh * HEAD_PAD, (hh + 1) * HEAD_PAD)
        m_prev = m_sc[qi, hh]
        m_next = jnp.maximum(m_prev, cm_sc[slot, hh])
        p = jnp.exp2(s_sc[slot, hh] - m_next[0:1, :]).astype(jnp.bfloat16)
        alpha = jnp.exp2(m_prev - m_next)
        acc_sc[qi, hh] = alpha[0:1, :] * acc_sc[qi, hh] + _dot(vt_ref[hs, tile(j)], p)
        m_sc[qi, hh] = m_next

    half = t // 2
    key = lax.broadcasted_iota(jnp.int32, (half, half), 0)
    qry = lax.broadcasted_iota(jnp.int32, (half, half), 1)
    causal = key <= qry

    def scores_diag(qi, slot, hh):
        hs = slice(hh * HEAD_PAD, (hh + 1) * HEAD_PAD)
        lo, mid_, hi = qi * t, qi * t + half, (qi + 1) * t
        top = _dot(k_ref[lo:mid_, hs], qt_ref[hs, lo:hi])
        bot = jnp.where(causal, _dot(k_ref[mid_:hi, hs], qt_ref[hs, mid_:hi]), MASK_VALUE)
        top_left = jnp.where(causal, top[:, :half], MASK_VALUE)
        s_sc[slot, hh, 0:half, 0:half] = top_left
        s_sc[slot, hh, 0:half, half:t] = top[:, half:]
        s_sc[slot, hh, half:t, half:t] = bot
        cm = jnp.concatenate([jnp.max(top_left, axis=0, keepdims=True),
                              jnp.maximum(jnp.max(top[:, half:], axis=0, keepdims=True),
                                          jnp.max(bot, axis=0, keepdims=True))], axis=1)
        cm_sc[slot, hh] = jnp.broadcast_to(cm, (SUBLANES, t))

    def update_diag(qi, slot, hh):
        hs = slice(hh * HEAD_PAD, (hh + 1) * HEAD_PAD)
        lo, mid_, hi = qi * t, qi * t + half, (qi + 1) * t
        m_prev = m_sc[qi, hh]
        m_next = jnp.maximum(m_prev, cm_sc[slot, hh])
        p_top = jnp.exp2(s_sc[slot, hh, 0:half, :] - m_next[0:1, :]).astype(jnp.bfloat16)
        p_bot = jnp.exp2(s_sc[slot, hh, half:t, half:t] - m_next[0:1, half:]).astype(jnp.bfloat16)
        pv = _dot(vt_ref[hs, lo:mid_], p_top)
        pv = jnp.concatenate([pv[:, :half], pv[:, half:] + _dot(vt_ref[hs, mid_:hi], p_bot)], axis=1)
        return jnp.exp2(m_prev - m_next)[0:1, :] * acc_sc[qi, hh] + pv

    def off_step(n):
        return step_ref[0, n], step_ref[1, n]

    for hh in range(2):
        scores(*off_step(0), 0, hh)

    def body(p, carry):
        for slot in range(2):
            n = 2 * p + slot
            for hh in range(2):
                scores(*off_step(n + 1), 1 - slot, hh)
                update(*off_step(n), slot, hh)
        return carry

    lax.fori_loop(0, n_off // 2 - 1, body, 0)
    for hh in range(2):
        scores(*off_step(n_off - 1), 1, hh)
        update(*off_step(n_off - 2), 0, hh)
    for hh in range(2):
        scores_diag(0, 0, hh)
        update(*off_step(n_off - 1), 1, hh)

    lane = lax.broadcasted_iota(jnp.int32, (t, LANES), 1)
    for qi in range(nq):
        slot = qi % 2
        outs = []
        for hh in range(2):
            if qi + 1 < nq:
                scores_diag(qi + 1, 1 - slot, hh)
            acc = update_diag(qi, slot, hh)
            outs.append((acc / acc[MLA_V:MLA_V + 1, :]).T)
        pair = jnp.where(lane < MLA_V, outs[0], pltpu.roll(outs[1], MLA_V, 1))
        o_ref[qi * t:(qi + 1) * t, :] = pair.astype(o_ref.dtype)


def _attention(qt, k, vt, bsz, seq):
    t = ATTN_TILE
    nq = seq // t
    pairs = MLA_HEADS // 2
    steps = jnp.asarray([[qi for qi in range(nq) for _ in range(qi)],
                         [j for qi in range(nq) for j in range(qi)]], jnp.int32)
    grid_spec = pltpu.PrefetchScalarGridSpec(
        num_scalar_prefetch=1,
        grid=(bsz, pairs),
        in_specs=[
            pl.BlockSpec((2 * HEAD_PAD, seq), lambda bi, p, steps_ref: (p, bi)),
            pl.BlockSpec((None, seq, 2 * HEAD_PAD), lambda bi, p, steps_ref: (bi, 0, p)),
            pl.BlockSpec((2 * HEAD_PAD, seq), lambda bi, p, steps_ref: (p, bi)),
        ],
        out_specs=pl.BlockSpec((None, seq, 2 * MLA_V), lambda bi, p, steps_ref: (bi, 0, p)),
        scratch_shapes=[
            pltpu.VMEM((2, 2, t, t), jnp.float32),
            pltpu.VMEM((2, 2, SUBLANES, t), jnp.float32),
            pltpu.VMEM((nq, 2, SUBLANES, t), jnp.float32),
            pltpu.VMEM((nq, 2, HEAD_PAD, t), jnp.float32),
        ],
    )
    return pl.pallas_call(
        _attn_kernel,
        grid_spec=grid_spec,
        out_shape=jax.ShapeDtypeStruct((bsz, seq, MLA_HEADS * MLA_V), jnp.bfloat16),
        compiler_params=_params(("parallel", "parallel")),
        name="mla_attention",
    )(steps, qt, k, vt)


def _post_kernel(n_mix, *refs):
    h_ref = refs[0]
    mix_refs = refs[1:1 + n_mix]
    wout_refs = refs[1 + n_mix:1 + 2 * n_mix]
    g1_ref, b1_ref, w1_ref, w2_ref, g2_ref, b2_ref, o_ref = refs[1 + 2 * n_mix:]
    tm = h_ref.shape[0]
    halves = [slice(r, r + POST_CHAIN_ROWS) for r in range(0, tm, POST_CHAIN_ROWS)]
    mixes = []
    for rows in halves:
        mix = _dot(mix_refs[0][rows, :], wout_refs[0][...])
        for m_ref, w_ref in zip(mix_refs[1:], wout_refs[1:]):
            mix = mix + _dot(m_ref[rows, :], w_ref[...])
        mixes.append(mix)
    for rows, mix in zip(halves, mixes):
        y = _layer_norm(DN_ALPHA * h_ref[rows, :] + mix, g1_ref[...], b1_ref[...])
        yb = y.astype(jnp.bfloat16)
        ff = None
        for c in range(D_FF // FF_CHUNK):
            cs = slice(c * FF_CHUNK, (c + 1) * FF_CHUNK)
            a = jnp.maximum(_dot(yb, w1_ref[:, cs]), 0.0)
            part = _dot((a * a).astype(jnp.bfloat16), w2_ref[cs, :])
            ff = part if ff is None else ff + part
        o_ref[rows, :] = _layer_norm(DN_ALPHA * y + ff, g2_ref[...], b2_ref[...])


def _post(layer, h2, mixes, wouts, g1, b1, w1, w2, g2, b2):
    t = h2.shape[0]
    tm = POST_TILE
    row = lambda i: (i, 0)
    n = len(mixes)
    layer_slab = lambda a: pl.BlockSpec((None,) + a.shape[1:], lambda i: (layer, 0, 0), pipeline_mode=pl.Buffered(1))
    in_specs = [pl.BlockSpec((tm, D_MODEL), row)]
    in_specs += [pl.BlockSpec((tm, m.shape[1]), row) for m in mixes]
    in_specs += [_resident(w.shape) for w in wouts]
    in_specs += [_resident(g1.shape), _resident(b1.shape), layer_slab(w1), layer_slab(w2), _resident(g2.shape),
                 _resident(b2.shape)]
    return pl.pallas_call(
        functools.partial(_post_kernel, n),
        grid=(t // tm,),
        in_specs=in_specs,
        out_specs=pl.BlockSpec((tm, D_MODEL), row),
        out_shape=jax.ShapeDtypeStruct((t, D_MODEL), jnp.float32),
        compiler_params=_params(("parallel",)),
        name="post",
    )(h2, *mixes, *wouts, g1, b1, w1, w2, g2, b2)


HG_SLABS = ("q", "k", "v", "gate")
HG_SLAB = {name: slice(i * HG_DK, (i + 1) * HG_DK) for i, name in enumerate(HG_SLABS)}
HG_WIDE = len(HG_SLABS) * HG_DK


def _odd_front_kernel(x_ref, w_ref, lb_ref, gn_ref, o_ref, lg_ref, risk_ref):
    tm = x_ref.shape[0]
    half = HG_CHUNK // 2
    xb = x_ref[...].astype(jnp.bfloat16)
    lbp = lb_ref[...]
    mx = jnp.max(lbp, axis=0, keepdims=True)
    e = jnp.exp(lbp - mx)
    sm = e / jnp.sum(e, axis=0, keepdims=True)
    lb_all = (sm[0:1, :] + sm[1:2, :]) - sm[0:1, :]
    heads_per_dot = 2
    wide = HG_HEADS * HG_DK
    for h in range(HG_HEADS):
        hs = slice(h * HG_DK, (h + 1) * HG_DK)
        if h % heads_per_dot == 0:
            zs = [_dot(xb, w_ref[:, s * wide + h * HG_DK:s * wide + (h + heads_per_dot) * HG_DK]) for s in range(4)]
        sub = slice((h % heads_per_dot) * HG_DK, (h % heads_per_dot + 1) * HG_DK)
        zq, zf, zi, zg = (zz[:, sub] for zz in zs)
        c_lb = 0.5 * (1.0 - lb_all[:, hs])
        ct = c_lb * jnp.tanh(0.5 * zf)
        gate = (1.0 - c_lb) + ct
        lg = jnp.log2(gate)
        lg_ref[:, hs] = lg
        hq = 0.5 * zq
        hg = 0.5 * zg
        base = h * HG_WIDE
        slabs = {"q": hq + hq * jnp.tanh(hq), "k": c_lb - ct, "v": zi,
                 "gate": gn_ref[:, hs] * (hg + hg * jnp.tanh(hg))}
        for name, val in slabs.items():
            sl = HG_SLAB[name]
            o_ref[:, base + sl.start:base + sl.stop] = val.astype(o_ref.dtype)
        risk = None
        for r0 in range(0, tm, half):
            s = -jnp.sum(lg[r0:r0 + half], axis=0, keepdims=True)
            risk = s if risk is None else jnp.maximum(risk, s)
        risk_ref[h:h + 1, :] = jnp.broadcast_to(jnp.max(risk, axis=1, keepdims=True), (1, LANES))


def _odd_front(x2, w, lb, gn):
    t = x2.shape[0]
    tm = TOKEN_TILE
    return pl.pallas_call(
        _odd_front_kernel,
        grid=(t // tm,),
        in_specs=[pl.BlockSpec((tm, x2.shape[1]), lambda i: (i, 0)), _resident(w.shape), _resident(lb.shape),
                  _resident(gn.shape)],
        out_specs=(pl.BlockSpec((tm, HG_HEADS * HG_WIDE), lambda i: (i, 0)),
                   pl.BlockSpec((tm, HG_HEADS * HG_DK), lambda i: (i, 0)),
                   pl.BlockSpec((None, HG_HEADS, LANES), lambda i: (i, 0, 0))),
        out_shape=(jax.ShapeDtypeStruct((t, HG_HEADS * HG_WIDE), jnp.bfloat16),
                   jax.ShapeDtypeStruct((t, HG_HEADS * HG_DK), jnp.float32),
                   jax.ShapeDtypeStruct((t // tm, HG_HEADS, LANES), jnp.float32)),
        compiler_params=_params(("parallel",)),
        name="odd_front",
    )(x2, w, lb, gn)


def _midpoint_rows(b, half):
    c = b.shape[0]
    if half >= SUBLANES:
        parts = []
        for p in range(c // (2 * half)):
            r = p * 2 * half + half - 1
            parts.append(jnp.broadcast_to(b[r:r + 1, :], (2 * half, LANES)))
        return parts[0] if len(parts) == 1 else jnp.concatenate(parts, axis=0)
    b3 = b.reshape(c // SUBLANES, SUBLANES, LANES)
    sub = lax.broadcasted_iota(jnp.int32, b3.shape, 1)
    out = None
    for p in range(SUBLANES // (2 * half)):
        r = p * 2 * half + half - 1
        cand = jnp.broadcast_to(b3[:, r:r + 1, :], b3.shape)
        out = cand if out is None else jnp.where(sub >= p * 2 * half, cand, out)
    return out.reshape(c, LANES)


def _hgrn_kernel(safe_ref, z_ref, lg_ref, tril_ref, o_ref, st_ref):
    c = HG_CHUNK
    mid = c // 2 - 1
    n_chunks = z_ref.shape[0] // c
    n_levels = int(math.log2(c))
    bi, h, li = pl.program_id(0), pl.program_id(1), pl.program_id(2)
    safe = safe_ref[bi * pl.num_programs(2) + li, h] != 0

    @pl.when(li == 0)
    def _():
        st_ref[...] = jnp.zeros(st_ref.shape, jnp.float32)

    tril = tril_ref[...]

    def slab(rs, name):
        return z_ref[rs, HG_SLAB[name]]

    def cumsum(rs):
        lg = lg_ref[rs, :]
        hi = lg.astype(jnp.bfloat16)
        lo = (lg - hi.astype(jnp.float32)).astype(jnp.bfloat16)
        return _dot(tril, hi) + _dot(tril, lo)

    def load_qk(rs):
        return slab(rs, "q").astype(jnp.float32), slab(rs, "k").astype(jnp.float32)

    t_idx = lax.broadcasted_iota(jnp.int32, (c, c), 0)
    s_idx = lax.broadcasted_iota(jnp.int32, (c, c), 1)

    def state_update(rs, kd, b_last):
        upd = lax.dot_general(slab(rs, "v"), kd, (((0,), (0,)), ((), ())), preferred_element_type=jnp.float32)
        return jnp.exp2(b_last), upd

    def chunk_output(rs, qd, st, attn):
        o = _dot_nt(qd, st.astype(jnp.bfloat16))
        o = o + _dot(attn.astype(jnp.bfloat16), slab(rs, "v"))
        o = o * lax.rsqrt(jnp.mean(o * o, axis=-1, keepdims=True) + NORM_EPS)
        o_ref[rs, :] = (o * slab(rs, "gate").astype(jnp.float32)).astype(o_ref.dtype)

    @pl.when(safe)
    def _():
        causal = s_idx <= t_idx
        chunks = [slice(ci * c, (ci + 1) * c) for ci in range(n_chunks)]
        bs = [cumsum(rs) for rs in chunks]
        scores, decays, upds, qds = [], [], [], []
        for rs, b in zip(chunks, bs):
            q, k = load_qk(rs)
            b_mid, b_last = b[mid:mid + 1, :], b[c - 1:c, :]
            d = b - b_mid
            qe = q * jnp.exp2(d)
            ke = k * jnp.exp2(-d)
            scores.append(_dot_nt(qe.astype(jnp.bfloat16), ke.astype(jnp.bfloat16)))
            decay, upd = state_update(rs, (ke * jnp.exp2(b_last - b_mid)).astype(jnp.bfloat16), b_last)
            decays.append(decay)
            upds.append(upd)
            qds.append((qe * jnp.exp2(b_mid)).astype(jnp.bfloat16))
        states = [st_ref[...]]
        for ci in range(n_chunks):
            states.append(states[-1] * decays[ci] + upds[ci])
        st_ref[...] = states[-1]
        for ci, rs in enumerate(chunks):
            chunk_output(rs, qds[ci], states[ci], jnp.where(causal, scores[ci], 0.0))

    @pl.when(jnp.logical_not(safe))
    def _():
        x = jnp.bitwise_xor(t_idx, s_idx)
        level = jnp.zeros((c, c), jnp.int32)
        for bit in range(n_levels):
            level = level + (x >= (1 << bit)).astype(jnp.int32)
        level = jnp.where(s_idx <= t_idx, level, -1)
        row_id = lax.broadcasted_iota(jnp.int32, (c, LANES), 0)

        def chunk_body(ci, carry):
            rs = pl.ds(pl.multiple_of(ci * c, c), c)
            (q, k), b = load_qk(rs), cumsum(rs)
            b_last = b[c - 1:c, :]
            attn = jnp.where(level == 0, _dot_nt(q.astype(jnp.bfloat16), k.astype(jnp.bfloat16)), 0.0)
            for lv in range(1, n_levels + 1):
                half = 1 << (lv - 1)
                right = jnp.bitwise_and(row_id, half) != 0
                d = b - _midpoint_rows(b, half)
                decay = jnp.exp2(jnp.where(right, d, -d))
                zl = (jnp.where(right, q, k) * decay).astype(jnp.bfloat16)
                attn = jnp.where(level == lv, _dot_nt(zl, zl), attn)
            st = st_ref[...]
            decay, upd = state_update(rs, (k * jnp.exp2(b_last - b)).astype(jnp.bfloat16), b_last)
            chunk_output(rs, (q * jnp.exp2(b)).astype(jnp.bfloat16), st, attn)
            st_ref[...] = st * decay + upd
            return carry

        lax.fori_loop(0, n_chunks, chunk_body, 0)


def _hgrn(safe, z, lg, tril):
    b, s, _ = z.shape
    blk = HG_BLOCK
    grid_spec = pltpu.PrefetchScalarGridSpec(
        num_scalar_prefetch=1,
        grid=(b, HG_HEADS, s // blk),
        in_specs=[
            pl.BlockSpec((None, blk, HG_WIDE), lambda bi, h, li, safe_ref: (bi, li, h)),
            pl.BlockSpec((None, blk, HG_DK), lambda bi, h, li, safe_ref: (bi, li, h)),
            pl.BlockSpec(tril.shape, lambda bi, h, li, safe_ref: (0, 0), pipeline_mode=pl.Buffered(1)),
        ],
        out_specs=pl.BlockSpec((None, blk, HG_DV), lambda bi, h, li, safe_ref: (bi, li, h)),
        scratch_shapes=[pltpu.VMEM((HG_DV, HG_DK), jnp.float32)],
    )
    return pl.pallas_call(
        _hgrn_kernel,
        grid_spec=grid_spec,
        out_shape=jax.ShapeDtypeStruct((b, s, HG_HEADS * HG_DV), jnp.bfloat16),
        compiler_params=_params(("parallel", "parallel", "arbitrary")),
        name="hgrn2",
    )(safe, z, lg, tril)


def kernel(x, positions, w_in_e, mla_gq, mla_gkv, w_qb, w_kvb, sgu_ln_g, sgu_ln_b, sgu_w, sgu_b, w_out_e,
           w_in_o, hg_lb, hg_gnorm, w_out_o, ln1_g, ln1_b, w_ff1, w_ff2, ln2_g, ln2_b):
    bsz, seq, d = x.shape
    assert d == D_MODEL and hg_lb.shape[0] == DEPTH == 2
    assert seq % ATTN_TILE == 0 and seq % HG_BLOCK == 0 and (bsz * seq) % TOKEN_TILE == 0
    t = bsz * seq
    bf = jnp.bfloat16
    f32 = jnp.float32
    row = lambda a: a.reshape(1, -1).astype(f32)

    inv_freq = ROPE_BASE ** (-jnp.arange(ROPE_HALF, dtype=f32) / ROPE_HALF)
    freq_col = jnp.broadcast_to(inv_freq[:, None], (ROPE_HALF, LANES))

    we = w_in_e[0].astype(bf)
    s0, s1, s2, s3 = MLA_LORA, 2 * MLA_LORA, 2 * MLA_LORA + MLA_ROPE, 2 * MLA_LORA + MLA_ROPE + SGU_DIM
    kr_cols = jnp.pad(we[:, s1:s2], ((0, 0), (ROPE_LO, HEAD_PAD - ROPE_LO - MLA_ROPE)))
    w_in_p = jnp.concatenate([we[:, :s1], kr_cols, we[:, s2:s3], we[:, s3:]], axis=1)
    assert w_in_p.shape[1] == EVEN_COLS
    dq = MLA_NOPE + MLA_ROPE
    wq_p = jnp.pad(w_qb[0].reshape(MLA_LORA, MLA_HEADS, dq), ((0, 0), (0, 0), (0, HEAD_PAD - dq)))
    wqt_p = wq_p.reshape(MLA_LORA, MLA_HEADS * HEAD_PAD).T.astype(bf)
    wkv = w_kvb[0].reshape(MLA_LORA, MLA_HEADS, MLA_NOPE + MLA_V)
    wk_p = jnp.pad(wkv[:, :, :MLA_NOPE], ((0, 0), (0, 0), (0, HEAD_PAD - MLA_NOPE)))
    wk_p = wk_p.reshape(MLA_LORA, MLA_HEADS * HEAD_PAD).astype(bf)
    wv_p = jnp.pad(wkv[:, :, MLA_NOPE:], ((0, 0), (0, 0), (0, HEAD_PAD - MLA_V)))
    wvt_p = wv_p.reshape(MLA_LORA, MLA_HEADS * HEAD_PAD).T.astype(bf)
    one_col = jnp.tile(jnp.zeros((HEAD_PAD,), f32).at[MLA_V].set(1.0), MLA_HEADS)
    one_col = jnp.broadcast_to(one_col[:, None], (MLA_HEADS * HEAD_PAD, LANES))
    sgu_bias = jnp.repeat(sgu_b[0].T, SGU_GROUP_DIM, axis=1).astype(f32)

    x2 = x.reshape(t, D_MODEL)
    qt, k, vt, b_out = _even_front(x2, positions.reshape(1, t), freq_col, w_in_p, row(mla_gq[0]), row(mla_gkv[0]),
                                   wqt_p, wk_p, wvt_p, one_col, row(sgu_ln_g[0]), row(sgu_ln_b[0]),
                                   sgu_w[0].astype(f32), sgu_bias)
    a_out = _attention(qt, k.reshape(bsz, seq, -1), vt, bsz, seq)
    a_rows = MLA_HEADS * MLA_V
    w1_bf, w2_bf = w_ff1.astype(bf), w_ff2.astype(bf)
    h1 = _post(0, x2, [a_out.reshape(t, -1), b_out], [w_out_e[0][:a_rows].astype(bf), w_out_e[0][a_rows:].astype(bf)],
               row(ln1_g[0]), row(ln1_b[0]), w1_bf, w2_bf, row(ln2_g[0]), row(ln2_b[0]))

    z, lg, risk = _odd_front(h1, w_in_o[0].astype(bf), hg_lb.astype(f32), row(hg_gnorm[0]))
    risk = risk[:, :, 0].reshape(t // HG_BLOCK, HG_BLOCK // TOKEN_TILE, HG_HEADS).max(axis=1)
    safe = (risk <= HG_SAFE_EXPONENT * LOG2_E).astype(jnp.int32)
    tril = jnp.tril(jnp.ones((HG_CHUNK, HG_CHUNK), bf))
    o = _hgrn(safe, z.reshape(bsz, seq, -1), lg.reshape(bsz, seq, -1), tril)
    h2 = _post(1, h1, [o.reshape(t, -1)], [w_out_o[0].astype(bf)], row(ln1_g[1]), row(ln1_b[1]), w1_bf, w2_bf,
               row(ln2_g[1]), row(ln2_b[1]))
    return h2.reshape(bsz, seq, D_MODEL)
```

```python
import functools
import math

import jax
import jax.numpy as jnp
from jax import lax
from jax.experimental import pallas as pl
from jax.experimental.pallas import tpu as pltpu

D_MODEL = 1024
DEPTH = 2
MLA_HEADS = 8
MLA_LORA = 256
MLA_NOPE = 64
MLA_ROPE = 32
MLA_V = 64
MLA_SCALE = (MLA_NOPE + MLA_ROPE) ** -0.5
LOG2_E = math.log2(math.e)
ROPE_BASE = 10000.0
SGU_GROUPS = 4
SGU_GROUP_DIM = 128
SGU_DIM = SGU_GROUPS * SGU_GROUP_DIM
SGU_CHUNK = 128
HG_HEADS = 8
HG_DK = 128
HG_DV = 128
D_FF = 4 * D_MODEL
DN_ALPHA = (2 * DEPTH) ** 0.25
NORM_EPS = 1e-5

LANES = 128
SUBLANES = 8
VMEM_LIMIT_BYTES = 56 * 1024 * 1024

TOKEN_TILE = 1024
EVEN_TILE = 1024
FRONT_CHAIN_ROWS = 256
POST_TILE = 1024
POST_CHAIN_ROWS = 256
ATTN_TILE = 512
HG_CHUNK = 128
HG_BLOCK = 4096
FF_CHUNK = 1024
MASK_VALUE = -1e30
HG_SAFE_EXPONENT = 60.0

HEAD_PAD = LANES
COL_CQ = 0
COL_CKV = COL_CQ + MLA_LORA
COL_KR = COL_CKV + MLA_LORA
COL_U = COL_KR + HEAD_PAD
COL_V = COL_U + SGU_DIM
EVEN_COLS = COL_V + SGU_DIM
ROPE_HALF = MLA_ROPE // 2
ROPE_LO = MLA_NOPE


def _params(semantics):
    return pltpu.CompilerParams(dimension_semantics=semantics, vmem_limit_bytes=VMEM_LIMIT_BYTES)


def _resident(shape):
    zeros = (0,) * len(shape)
    return pl.BlockSpec(shape, lambda *_: zeros, pipeline_mode=pl.Buffered(1))


def _dot(a, b):
    return jnp.dot(a, b, preferred_element_type=jnp.float32)


def _dot_nt(a, b):
    return lax.dot_general(a, b, (((1,), (1,)), ((), ())), preferred_element_type=jnp.float32)


def _gelu_tanh(x):
    c = math.sqrt(2.0 / math.pi)
    return 0.5 * x * (1.0 + jnp.tanh(c * (x + 0.044715 * (x * x * x))))


def _layer_norm(x, g, b):
    mu = jnp.mean(x, axis=-1, keepdims=True)
    xc = x - mu
    var = jnp.mean(xc * xc, axis=-1, keepdims=True)
    return xc * lax.rsqrt(var + NORM_EPS) * g + b


def _rms_norm(x, g):
    return x * lax.rsqrt(jnp.mean(x * x, axis=-1, keepdims=True) + NORM_EPS) * g


def _even_front_kernel(x_ref, pos_ref, freq_ref, w_in_ref, gq_ref, gkv_ref, wqt_ref, wk_ref, wvt_ref, one_ref,
                       lng_ref, lnb_ref, sw_ref, sb_ref, qt_ref, k_ref, vt_ref, b_ref):
    chains = [slice(r, r + FRONT_CHAIN_ROWS) for r in range(0, x_ref.shape[0], FRONT_CHAIN_ROWS)]
    zs = [_dot(x_ref[rows, :].astype(jnp.bfloat16), w_in_ref[...]) for rows in chains]
    for rows, z in zip(chains, zs):
        _even_front_chain(rows, z, pos_ref, freq_ref, gq_ref, gkv_ref, wqt_ref, wk_ref, wvt_ref, one_ref,
                          lng_ref, lnb_ref, sw_ref, sb_ref, qt_ref, k_ref, vt_ref, b_ref)


def _even_front_chain(rows, z, pos_ref, freq_ref, gq_ref, gkv_ref, wqt_ref, wk_ref, wvt_ref, one_ref,
                      lng_ref, lnb_ref, sw_ref, sb_ref, qt_ref, k_ref, vt_ref, b_ref):
    tm = z.shape[0]
    rep = tm // LANES

    ang = jnp.tile(freq_ref[...], (1, rep)) * pos_ref[:, rows].astype(jnp.float32)
    cos_t = jnp.cos(ang)
    sin_t = jnp.sin(ang)

    cq = _rms_norm(z[:, COL_CQ:COL_CQ + MLA_LORA], gq_ref[...] * (MLA_SCALE * LOG2_E)).astype(jnp.bfloat16)
    ckv = _rms_norm(z[:, COL_CKV:COL_CKV + MLA_LORA], gkv_ref[...]).astype(jnp.bfloat16)

    qt = _dot_nt(wqt_ref[...], cq)
    for h in range(MLA_HEADS):
        base = h * HEAD_PAD
        x1 = qt[base + ROPE_LO:base + ROPE_LO + ROPE_HALF]
        x2 = qt[base + ROPE_LO + ROPE_HALF:base + ROPE_LO + MLA_ROPE]
        slab = jnp.concatenate([qt[base:base + ROPE_LO], x1 * cos_t - x2 * sin_t, x2 * cos_t + x1 * sin_t,
                                qt[base + ROPE_LO + MLA_ROPE:base + HEAD_PAD]], axis=0)
        qt_ref[base:base + HEAD_PAD, rows] = slab.astype(qt_ref.dtype)

    vt = _dot_nt(wvt_ref[...], ckv) + jnp.tile(one_ref[...], (1, rep))
    vt_ref[:, rows] = vt.astype(vt_ref.dtype)

    ones = jnp.ones((ROPE_LO, tm), jnp.float32)
    zeros = jnp.zeros((HEAD_PAD - ROPE_LO - MLA_ROPE, tm), jnp.float32)
    cos_k = jnp.concatenate([ones, cos_t, cos_t, zeros], axis=0).T
    sin_k = jnp.concatenate([0.0 * ones, -sin_t, sin_t, zeros], axis=0).T
    lane = lax.broadcasted_iota(jnp.int32, (tm, LANES), 1)
    kr = z[:, COL_KR:COL_KR + HEAD_PAD]
    partner = jnp.where(lane < ROPE_LO + ROPE_HALF, pltpu.roll(kr, LANES - ROPE_HALF, 1), pltpu.roll(kr, ROPE_HALF, 1))
    kr = kr * cos_k + partner * sin_k
    kn = _dot(ckv, wk_ref[...])
    for h in range(MLA_HEADS):
        sl = slice(h * HEAD_PAD, (h + 1) * HEAD_PAD)
        k_ref[rows, sl] = (kn[:, sl] + kr).astype(k_ref.dtype)

    u = _gelu_tanh(z[:, COL_U:COL_U + SGU_DIM])
    vn = _layer_norm(_gelu_tanh(z[:, COL_V:COL_V + SGU_DIM]), lng_ref[...], lnb_ref[...]).astype(jnp.bfloat16)
    row = lax.broadcasted_iota(jnp.int32, (SGU_CHUNK, SGU_CHUNK), 0)
    col = lax.broadcasted_iota(jnp.int32, (SGU_CHUNK, SGU_CHUNK), 1)
    causal = col <= row
    bias = sb_ref[...]
    for g in range(SGU_GROUPS):
        wg = jnp.where(causal, sw_ref[g], 0.0).astype(jnp.bfloat16)
        gs = slice(g * SGU_GROUP_DIM, (g + 1) * SGU_GROUP_DIM)
        for c in range(tm // SGU_CHUNK):
            rs = slice(c * SGU_CHUNK, (c + 1) * SGU_CHUNK)
            mixed = _dot(wg, vn[rs, gs]) + bias[:, gs]
            out_rows = slice(rows.start + rs.start, rows.start + rs.stop)
            b_ref[out_rows, gs] = (u[rs, gs] * mixed).astype(b_ref.dtype)


def _even_front(x2, pos_row, freq_col, w_in, gq, gkv, wqt, wk, wvt, one_col, lng, lnb, sw, sb):
    t = x2.shape[0]
    tm = EVEN_TILE
    row = lambda i: (i, 0)
    col = lambda i: (0, i)
    wide = MLA_HEADS * HEAD_PAD
    out_shapes = (
        jax.ShapeDtypeStruct((wide, t), jnp.bfloat16),
        jax.ShapeDtypeStruct((t, wide), jnp.bfloat16),
        jax.ShapeDtypeStruct((wide, t), jnp.bfloat16),
        jax.ShapeDtypeStruct((t, SGU_DIM), jnp.bfloat16),
    )
    consts = (freq_col, w_in, gq, gkv, wqt, wk, wvt, one_col, lng, lnb, sw, sb)
    return pl.pallas_call(
        _even_front_kernel,
        grid=(t // tm,),
        in_specs=[pl.BlockSpec((tm, D_MODEL), row), pl.BlockSpec((1, tm), col)] + [_resident(a.shape) for a in consts],
        out_specs=(
            pl.BlockSpec((wide, tm), col),
            pl.BlockSpec((tm, wide), row),
            pl.BlockSpec((wide, tm), col),
            pl.BlockSpec((tm, SGU_DIM), row),
        ),
        out_shape=out_shapes,
        compiler_params=_params(("parallel",)),
        name="even_front",
    )(x2, pos_row, *consts)


def _attn_kernel(step_ref, qt_ref, k_ref, vt_ref, o_ref, s_sc, cm_sc, m_sc, acc_sc):
    t = ATTN_TILE
    nq = qt_ref.shape[1] // t
    n_off = nq * (nq - 1) // 2
    assert n_off % 2 == 0 and n_off >= 2
    m_sc[...] = jnp.full(m_sc.shape, MASK_VALUE, jnp.float32)
    acc_sc[...] = jnp.zeros(acc_sc.shape, jnp.float32)

    def tile(i):
        return slice(i * t, (i + 1) * t) if isinstance(i, int) else pl.ds(pl.multiple_of(i * t, t), t)

    def scores(qi, j, slot, hh):
        hs = slice(hh * HEAD_PAD, (hh + 1) * HEAD_PAD)
        s = _dot(k_ref[tile(j), hs], qt_ref[hs, tile(qi)])
        s_sc[slot, hh] = s
        cm_sc[slot, hh] = jnp.broadcast_to(jnp.max(s, axis=0, keepdims=True), (SUBLANES, t))

    def update(qi, j, slot, hh):
        hs = slice(hh * HEAD_PAD, (hh + 1) * HEAD_PAD)
        m_prev = m_sc[qi, hh]
        m_next = jnp.maximum(m_prev, cm_sc[slot, hh])
        p = jnp.exp2(s_sc[slot, hh] - m_next[0:1, :]).astype(jnp.bfloat16)
        alpha = jnp.exp2(m_prev - m_next)
        acc_sc[qi, hh] = alpha[0:1, :] * acc_sc[qi, hh] + _dot(vt_ref[hs, tile(j)], p)
        m_sc[qi, hh] = m_next

    half = t // 2
    key = lax.broadcasted_iota(jnp.int32, (half, half), 0)
    qry = lax.broadcasted_iota(jnp.int32, (half, half), 1)
    causal = key <= qry

    def scores_diag(qi, slot, hh):
        hs = slice(hh * HEAD_PAD, (hh + 1) * HEAD_PAD)
        lo, mid_, hi = qi * t, qi * t + half, (qi + 1) * t
        top = _dot(k_ref[lo:mid_, hs], qt_ref[hs, lo:hi])
        bot = jnp.where(causal, _dot(k_ref[mid_:hi, hs], qt_ref[hs, mid_:hi]), MASK_VALUE)
        top_left = jnp.where(causal, top[:, :half], MASK_VALUE)
        s_sc[slot, hh, 0:half, 0:half] = top_left
        s_sc[slot, hh, 0:half, half:t] = top[:, half:]
        s_sc[slot, hh, half:t, half:t] = bot
        cm = jnp.concatenate([jnp.max(top_left, axis=0, keepdims=True),
                              jnp.maximum(jnp.max(top[:, half:], axis=0, keepdims=True),
                                          jnp.max(bot, axis=0, keepdims=True))], axis=1)
        cm_sc[slot, hh] = jnp.broadcast_to(cm, (SUBLANES, t))

    def update_diag(qi, slot, hh):
        hs = slice(hh * HEAD_PAD, (hh + 1) * HEAD_PAD)
        lo, mid_, hi = qi * t, qi * t + half, (qi + 1) * t
        m_prev = m_sc[qi, hh]
        m_next = jnp.maximum(m_prev, cm_sc[slot, hh])
        p_top = jnp.exp2(s_sc[slot, hh, 0:half, :] - m_next[0:1, :]).astype(jnp.bfloat16)
        p_bot = jnp.exp2(s_sc[slot, hh, half:t, half:t] - m_next[0:1, half:]).astype(jnp.bfloat16)
        pv = _dot(vt_ref[hs, lo:mid_], p_top)
        pv = jnp.concatenate([pv[:, :half], pv[:, half:] + _dot(vt_ref[hs, mid_:hi], p_bot)], axis=1)
        return jnp.exp2(m_prev - m_next)[0:1, :] * acc_sc[qi, hh] + pv

    def off_step(n):
        return step_ref[0, n], step_ref[1, n]

    for hh in range(2):
        scores(*off_step(0), 0, hh)

    def body(p, carry):
        for slot in range(2):
            n = 2 * p + slot
            for hh in range(2):
                scores(*off_step(n + 1), 1 - slot, hh)
                update(*off_step(n), slot, hh)
        return carry

    lax.fori_loop(0, n_off // 2 - 1, body, 0)
    for hh in range(2):
        scores(*off_step(n_off - 1), 1, hh)
        update(*off_step(n_off - 2), 0, hh)
    for hh in range(2):
        scores_diag(0, 0, hh)
        update(*off_step(n_off - 1), 1, hh)

    lane = lax.broadcasted_iota(jnp.int32, (t, LANES), 1)
    for qi in range(nq):
        slot = qi % 2
        outs = []
        for hh in range(2):
            if qi + 1 < nq:
                scores_diag(qi + 1, 1 - slot, hh)
            acc = update_diag(qi, slot, hh)
            outs.append((acc / acc[MLA_V:MLA_V + 1, :]).T)
        pair = jnp.where(lane < MLA_V, outs[0], pltpu.roll(outs[1], MLA_V, 1))
        o_ref[qi * t:(qi + 1) * t, :] = pair.astype(o_ref.dtype)


def _attention(qt, k, vt, bsz, seq):
    t = ATTN_TILE
    nq = seq // t
    pairs = MLA_HEADS // 2
    steps = jnp.asarray([[qi for qi in range(nq) for _ in range(qi)],
                         [j for qi in range(nq) for j in range(qi)]], jnp.int32)
    grid_spec = pltpu.PrefetchScalarGridSpec(
        num_scalar_prefetch=1,
        grid=(bsz, pairs),
        in_specs=[
            pl.BlockSpec((2 * HEAD_PAD, seq), lambda bi, p, steps_ref: (p, bi)),
            pl.BlockSpec((None, seq, 2 * HEAD_PAD), lambda bi, p, steps_ref: (bi, 0, p)),
            pl.BlockSpec((2 * HEAD_PAD, seq), lambda bi, p, steps_ref: (p, bi)),
        ],
        out_specs=pl.BlockSpec((None, seq, 2 * MLA_V), lambda bi, p, steps_ref: (bi, 0, p)),
        scratch_shapes=[
            pltpu.VMEM((2, 2, t, t), jnp.float32),
            pltpu.VMEM((2, 2, SUBLANES, t), jnp.float32),
            pltpu.VMEM((nq, 2, SUBLANES, t), jnp.float32),
            pltpu.VMEM((nq, 2, HEAD_PAD, t), jnp.float32),
        ],
    )
    return pl.pallas_call(
        _attn_kernel,
        grid_spec=grid_spec,
        out_shape=jax.ShapeDtypeStruct((bsz, seq, MLA_HEADS * MLA_V), jnp.bfloat16),
        compiler_params=_params(("parallel", "parallel")),
        name="mla_attention",
    )(steps, qt, k, vt)


def _post_kernel(n_mix, *refs):
    h_ref = refs[0]
    mix_refs = refs[1:1 + n_mix]
    wout_refs = refs[1 + n_mix:1 + 2 * n_mix]
    g1_ref, b1_ref, w1_ref, w2_ref, g2_ref, b2_ref, o_ref = refs[1 + 2 * n_mix:]
    tm = h_ref.shape[0]
    halves = [slice(r, r + POST_CHAIN_ROWS) for r in range(0, tm, POST_CHAIN_ROWS)]
    mixes = []
    for rows in halves:
        mix = _dot(mix_refs[0][rows, :], wout_refs[0][...])
        for m_ref, w_ref in zip(mix_refs[1:], wout_refs[1:]):
            mix = mix + _dot(m_ref[rows, :], w_ref[...])
        mixes.append(mix)
    for rows, mix in zip(halves, mixes):
        y = _layer_norm(DN_ALPHA * h_ref[rows, :] + mix, g1_ref[...], b1_ref[...])
        yb = y.astype(jnp.bfloat16)
        ff = None
        for c in range(D_FF // FF_CHUNK):
            cs = slice(c * FF_CHUNK, (c + 1) * FF_CHUNK)
            a = jnp.maximum(_dot(yb, w1_ref[:, cs]), 0.0)
            part = _dot((a * a).astype(jnp.bfloat16), w2_ref[cs, :])
            ff = part if ff is None else ff + part
        o_ref[rows, :] = _layer_norm(DN_ALPHA * y + ff, g2_ref[...], b2_ref[...])


def _post(layer, h2, mixes, wouts, g1, b1, w1, w2, g2, b2):
    t = h2.shape[0]
    tm = POST_TILE
    row = lambda i: (i, 0)
    n = len(mixes)
    layer_slab = lambda a: pl.BlockSpec((None,) + a.shape[1:], lambda i: (layer, 0, 0), pipeline_mode=pl.Buffered(1))
    in_specs = [pl.BlockSpec((tm, D_MODEL), row)]
    in_specs += [pl.BlockSpec((tm, m.shape[1]), row) for m in mixes]
    in_specs += [_resident(w.shape) for w in wouts]
    in_specs += [_resident(g1.shape), _resident(b1.shape), layer_slab(w1), layer_slab(w2), _resident(g2.shape),
                 _resident(b2.shape)]
    return pl.pallas_call(
        functools.partial(_post_kernel, n),
        grid=(t // tm,),
        in_specs=in_specs,
        out_specs=pl.BlockSpec((tm, D_MODEL), row),
        out_shape=jax.ShapeDtypeStruct((t, D_MODEL), jnp.float32),
        compiler_params=_params(("parallel",)),
        name="post",
    )(h2, *mixes, *wouts, g1, b1, w1, w2, g2, b2)


HG_SLABS = ("q", "k", "v", "gate")
HG_SLAB = {name: slice(i * HG_DK, (i + 1) * HG_DK) for i, name in enumerate(HG_SLABS)}
HG_WIDE = len(HG_SLABS) * HG_DK


def _odd_front_kernel(x_ref, w_ref, lb_ref, gn_ref, o_ref, lg_ref, risk_ref):
    tm = x_ref.shape[0]
    half = HG_CHUNK // 2
    xb = x_ref[...].astype(jnp.bfloat16)
    lbp = lb_ref[...]
    mx = jnp.max(lbp, axis=0, keepdims=True)
    e = jnp.exp(lbp - mx)
    sm = e / jnp.sum(e, axis=0, keepdims=True)
    lb_all = (sm[0:1, :] + sm[1:2, :]) - sm[0:1, :]
    heads_per_dot = 2
    wide = HG_HEADS * HG_DK
    for h in range(HG_HEADS):
        hs = slice(h * HG_DK, (h + 1) * HG_DK)
        if h % heads_per_dot == 0:
            zs = [_dot(xb, w_ref[:, s * wide + h * HG_DK:s * wide + (h + heads_per_dot) * HG_DK]) for s in range(4)]
        sub = slice((h % heads_per_dot) * HG_DK, (h % heads_per_dot + 1) * HG_DK)
        zq, zf, zi, zg = (zz[:, sub] for zz in zs)
        c_lb = 0.5 * (1.0 - lb_all[:, hs])
        ct = c_lb * jnp.tanh(0.5 * zf)
        gate = (1.0 - c_lb) + ct
        lg = jnp.log2(gate)
        lg_ref[:, hs] = lg
        hq = 0.5 * zq
        hg = 0.5 * zg
        base = h * HG_WIDE
        slabs = {"q": hq + hq * jnp.tanh(hq), "k": c_lb - ct, "v": zi,
                 "gate": gn_ref[:, hs] * (hg + hg * jnp.tanh(hg))}
        for name, val in slabs.items():
            sl = HG_SLAB[name]
            o_ref[:, base + sl.start:base + sl.stop] = val.astype(o_ref.dtype)
        risk = None
        for r0 in range(0, tm, half):
            s = -jnp.sum(lg[r0:r0 + half], axis=0, keepdims=True)
            risk = s if risk is None else jnp.maximum(risk, s)
        risk_ref[h:h + 1, :] = jnp.broadcast_to(jnp.max(risk, axis=1, keepdims=True), (1, LANES))


def _odd_front(x2, w, lb, gn):
    t = x2.shape[0]
    tm = TOKEN_TILE
    return pl.pallas_call(
        _odd_front_kernel,
        grid=(t // tm,),
        in_specs=[pl.BlockSpec((tm, x2.shape[1]), lambda i: (i, 0)), _resident(w.shape), _resident(lb.shape),
                  _resident(gn.shape)],
        out_specs=(pl.BlockSpec((tm, HG_HEADS * HG_WIDE), lambda i: (i, 0)),
                   pl.BlockSpec((tm, HG_HEADS * HG_DK), lambda i: (i, 0)),
                   pl.BlockSpec((None, HG_HEADS, LANES), lambda i: (i, 0, 0))),
        out_shape=(jax.ShapeDtypeStruct((t, HG_HEADS * HG_WIDE), jnp.bfloat16),
                   jax.ShapeDtypeStruct((t, HG_HEADS * HG_DK), jnp.float32),
                   jax.ShapeDtypeStruct((t // tm, HG_HEADS, LANES), jnp.float32)),
        compiler_params=_params(("parallel",)),
        name="odd_front",
    )(x2, w, lb, gn)


def _midpoint_rows(b, half):
    c = b.shape[0]
    if half >= SUBLANES:
        parts = []
        for p in range(c // (2 * half)):
            r = p * 2 * half + half - 1
            parts.append(jnp.broadcast_to(b[r:r + 1, :], (2 * half, LANES)))
        return parts[0] if len(parts) == 1 else jnp.concatenate(parts, axis=0)
    b3 = b.reshape(c // SUBLANES, SUBLANES, LANES)
    sub = lax.broadcasted_iota(jnp.int32, b3.shape, 1)
    out = None
    for p in range(SUBLANES // (2 * half)):
        r = p * 2 * half + half - 1
        cand = jnp.broadcast_to(b3[:, r:r + 1, :], b3.shape)
        out = cand if out is None else jnp.where(sub >= p * 2 * half, cand, out)
    return out.reshape(c, LANES)


def _hgrn_kernel(safe_ref, z_ref, lg_ref, tril_ref, o_ref, st_ref):
    c = HG_CHUNK
    mid = c // 2 - 1
    n_chunks = z_ref.shape[0] // c
    n_levels = int(math.log2(c))
    bi, h, li = pl.program_id(0), pl.program_id(1), pl.program_id(2)
    safe = safe_ref[bi * pl.num_programs(2) + li, h] != 0

    @pl.when(li == 0)
    def _():
        st_ref[...] = jnp.zeros(st_ref.shape, jnp.float32)

    tril = tril_ref[...]

    def slab(rs, name):
        return z_ref[rs, HG_SLAB[name]]

    def cumsum(rs):
        lg = lg_ref[rs, :]
        hi = lg.astype(jnp.bfloat16)
        lo = (lg - hi.astype(jnp.float32)).astype(jnp.bfloat16)
        return _dot(tril, hi) + _dot(tril, lo)

    def load_qk(rs):
        return slab(rs, "q").astype(jnp.float32), slab(rs, "k").astype(jnp.float32)

    t_idx = lax.broadcasted_iota(jnp.int32, (c, c), 0)
    s_idx = lax.broadcasted_iota(jnp.int32, (c, c), 1)

    def state_update(rs, kd, b_last):
        upd = lax.dot_general(slab(rs, "v"), kd, (((0,), (0,)), ((), ())), preferred_element_type=jnp.float32)
        return jnp.exp2(b_last), upd

    def chunk_output(rs, qd, st, attn):
        o = _dot_nt(qd, st.astype(jnp.bfloat16))
        o = o + _dot(attn.astype(jnp.bfloat16), slab(rs, "v"))
        o = o * lax.rsqrt(jnp.mean(o * o, axis=-1, keepdims=True) + NORM_EPS)
        o_ref[rs, :] = (o * slab(rs, "gate").astype(jnp.float32)).astype(o_ref.dtype)

    @pl.when(safe)
    def _():
        causal = s_idx <= t_idx
        chunks = [slice(ci * c, (ci + 1) * c) for ci in range(n_chunks)]
        bs = [cumsum(rs) for rs in chunks]
        scores, decays, upds, qds = [], [], [], []
        for rs, b in zip(chunks, bs):
            q, k = load_qk(rs)
            b_mid, b_last = b[mid:mid + 1, :], b[c - 1:c, :]
            d = b - b_mid
            qe = q * jnp.exp2(d)
            ke = k * jnp.exp2(-d)
            scores.append(_dot_nt(qe.astype(jnp.bfloat16), ke.astype(jnp.bfloat16)))
            decay, upd = state_update(rs, (ke * jnp.exp2(b_last - b_mid)).astype(jnp.bfloat16), b_last)
            decays.append(decay)
            upds.append(upd)
            qds.append((qe * jnp.exp2(b_mid)).astype(jnp.bfloat16))
        states = [st_ref[...]]
        for ci in range(n_chunks):
            states.append(states[-1] * decays[ci] + upds[ci])
        st_ref[...] = states[-1]
        for ci, rs in enumerate(chunks):
            chunk_output(rs, qds[ci], states[ci], jnp.where(causal, scores[ci], 0.0))

    @pl.when(jnp.logical_not(safe))
    def _():
        x = jnp.bitwise_xor(t_idx, s_idx)
        level = jnp.zeros((c, c), jnp.int32)
        for bit in range(n_levels):
            level = level + (x >= (1 << bit)).astype(jnp.int32)
        level = jnp.where(s_idx <= t_idx, level, -1)
        row_id = lax.broadcasted_iota(jnp.int32, (c, LANES), 0)

        def chunk_body(ci, carry):
            rs = pl.ds(pl.multiple_of(ci * c, c), c)
            (q, k), b = load_qk(rs), cumsum(rs)
            b_last = b[c - 1:c, :]
            attn = jnp.where(level == 0, _dot_nt(q.astype(jnp.bfloat16), k.astype(jnp.bfloat16)), 0.0)
            for lv in range(1, n_levels + 1):
                half = 1 << (lv - 1)
                right = jnp.bitwise_and(row_id, half) != 0
                d = b - _midpoint_rows(b, half)
                decay = jnp.exp2(jnp.where(right, d, -d))
                zl = (jnp.where(right, q, k) * decay).astype(jnp.bfloat16)
                attn = jnp.where(level == lv, _dot_nt(zl, zl), attn)
            st = st_ref[...]
            decay, upd = state_update(rs, (k * jnp.exp2(b_last - b)).astype(jnp.bfloat16), b_last)
            chunk_output(rs, (q * jnp.exp2(b)).astype(jnp.bfloat16), st, attn)
            st_ref[...] = st * decay + upd
            return carry

        lax.fori_loop(0, n_chunks, chunk_body, 0)


def _hgrn(safe, z, lg, tril):
    b, s, _ = z.shape
    blk = HG_BLOCK
    grid_spec = pltpu.PrefetchScalarGridSpec(
        num_scalar_prefetch=1,
        grid=(b, HG_HEADS, s // blk),
        in_specs=[
            pl.BlockSpec((None, blk, HG_WIDE), lambda bi, h, li, safe_ref: (bi, li, h)),
            pl.BlockSpec((None, blk, HG_DK), lambda bi, h, li, safe_ref: (bi, li, h)),
            pl.BlockSpec(tril.shape, lambda bi, h, li, safe_ref: (0, 0), pipeline_mode=pl.Buffered(1)),
        ],
        out_specs=pl.BlockSpec((None, blk, HG_DV), lambda bi, h, li, safe_ref: (bi, li, h)),
        scratch_shapes=[pltpu.VMEM((HG_DV, HG_DK), jnp.float32)],
    )
    return pl.pallas_call(
        _hgrn_kernel,
        grid_spec=grid_spec,
        out_shape=jax.ShapeDtypeStruct((b, s, HG_HEADS * HG_DV), jnp.bfloat16),
        compiler_params=_params(("parallel", "parallel", "arbitrary")),
        name="hgrn2",
    )(safe, z, lg, tril)


def kernel(x, positions, w_in_e, mla_gq, mla_gkv, w_qb, w_kvb, sgu_ln_g, sgu_ln_b, sgu_w, sgu_b, w_out_e,
           w_in_o, hg_lb, hg_gnorm, w_out_o, ln1_g, ln1_b, w_ff1, w_ff2, ln2_g, ln2_b):
    bsz, seq, d = x.shape
    assert d == D_MODEL and hg_lb.shape[0] == DEPTH == 2
    assert seq % ATTN_TILE == 0 and seq % HG_BLOCK == 0 and (bsz * seq) % TOKEN_TILE == 0
    t = bsz * seq
    bf = jnp.bfloat16
    f32 = jnp.float32
    row = lambda a: a.reshape(1, -1).astype(f32)

    inv_freq = ROPE_BASE ** (-jnp.arange(ROPE_HALF, dtype=f32) / ROPE_HALF)
    freq_col = jnp.broadcast_to(inv_freq[:, None], (ROPE_HALF, LANES))

    we = w_in_e[0].astype(bf)
    s0, s1, s2, s3 = MLA_LORA, 2 * MLA_LORA, 2 * MLA_LORA + MLA_ROPE, 2 * MLA_LORA + MLA_ROPE + SGU_DIM
    kr_cols = jnp.pad(we[:, s1:s2], ((0, 0), (ROPE_LO, HEAD_PAD - ROPE_LO - MLA_ROPE)))
    w_in_p = jnp.concatenate([we[:, :s1], kr_cols, we[:, s2:s3], we[:, s3:]], axis=1)
    assert w_in_p.shape[1] == EVEN_COLS
    dq = MLA_NOPE + MLA_ROPE
    wq_p = jnp.pad(w_qb[0].reshape(MLA_LORA, MLA_HEADS, dq), ((0, 0), (0, 0), (0, HEAD_PAD - dq)))
    wqt_p = wq_p.reshape(MLA_LORA, MLA_HEADS * HEAD_PAD).T.astype(bf)
    wkv = w_kvb[0].reshape(MLA_LORA, MLA_HEADS, MLA_NOPE + MLA_V)
    wk_p = jnp.pad(wkv[:, :, :MLA_NOPE], ((0, 0), (0, 0), (0, HEAD_PAD - MLA_NOPE)))
    wk_p = wk_p.reshape(MLA_LORA, MLA_HEADS * HEAD_PAD).astype(bf)
    wv_p = jnp.pad(wkv[:, :, MLA_NOPE:], ((0, 0), (0, 0), (0, HEAD_PAD - MLA_V)))
    wvt_p = wv_p.reshape(MLA_LORA, MLA_HEADS * HEAD_PAD).T.astype(bf)
    one_col = jnp.tile(jnp.zeros((HEAD_PAD,), f32).at[MLA_V].set(1.0), MLA_HEADS)
    one_col = jnp.broadcast_to(one_col[:, None], (MLA_HEADS * HEAD_PAD, LANES))
    sgu_bias = jnp.repeat(sgu_b[0].T, SGU_GROUP_DIM, axis=1).astype(f32)

    x2 = x.reshape(t, D_MODEL)
    qt, k, vt, b_out = _even_front(x2, positions.reshape(1, t), freq_col, w_in_p, row(mla_gq[0]), row(mla_gkv[0]),
                                   wqt_p, wk_p, wvt_p, one_col, row(sgu_ln_g[0]), row(sgu_ln_b[0]),
                                   sgu_w[0].astype(f32), sgu_bias)
    a_out = _attention(qt, k.reshape(bsz, seq, -1), vt, bsz, seq)
    a_rows = MLA_HEADS * MLA_V
    w1_bf, w2_bf = w_ff1.astype(bf), w_ff2.astype(bf)
    h1 = _post(0, x2, [a_out.reshape(t, -1), b_out], [w_out_e[0][:a_rows].astype(bf), w_out_e[0][a_rows:].astype(bf)],
               row(ln1_g[0]), row(ln1_b[0]), w1_bf, w2_bf, row(ln2_g[0]), row(ln2_b[0]))

    z, lg, risk = _odd_front(h1, w_in_o[0].astype(bf), hg_lb.astype(f32), row(hg_gnorm[0]))
    risk = risk[:, :, 0].reshape(t // HG_BLOCK, HG_BLOCK // TOKEN_TILE, HG_HEADS).max(axis=1)
    safe = (risk <= HG_SAFE_EXPONENT * LOG2_E).astype(jnp.int32)
    tril = jnp.tril(jnp.ones((HG_CHUNK, HG_CHUNK), bf))
    o = _hgrn(safe, z.reshape(bsz, seq, -1), lg.reshape(bsz, seq, -1), tril)
    h2 = _post(1, h1, [o.reshape(t, -1)], [w_out_o[0].astype(bf)], row(ln1_g[1]), row(ln1_b[1]), w1_bf, w2_bf,
               row(ln2_g[1]), row(ln2_b[1]))
    return h2.reshape(bsz, seq, D_MODEL)
```

```python
import functools
import math

import jax
import jax.numpy as jnp
from jax import lax
from jax.experimental import pallas as pl
from jax.experimental.pallas import tpu as pltpu

D_MODEL = 1024
DEPTH = 2
MLA_HEADS = 8
MLA_LORA = 256
MLA_NOPE = 64
MLA_ROPE = 32
MLA_V = 64
MLA_SCALE = (MLA_NOPE + MLA_ROPE) ** -0.5
LOG2_E = math.log2(math.e)
ROPE_BASE = 10000.0
SGU_GROUPS = 4
SGU_GROUP_DIM = 128
SGU_DIM = SGU_GROUPS * SGU_GROUP_DIM
SGU_CHUNK = 128
HG_HEADS = 8
HG_DK = 128
HG_DV = 128
D_FF = 4 * D_MODEL
DN_ALPHA = (2 * DEPTH) ** 0.25
NORM_EPS = 1e-5

LANES = 128
SUBLANES = 8
VMEM_LIMIT_BYTES = 56 * 1024 * 1024

TOKEN_TILE = 1024
EVEN_TILE = 1024
FRONT_CHAIN_ROWS = 256
POST_TILE = 1024
POST_CHAIN_ROWS = 256
ATTN_TILE = 512
ATTN_STEPS_PER_TRIP = 4
HG_CHUNK = 128
HG_BLOCK = 4096
FF_CHUNK = 1024
MASK_VALUE = -1e30
HG_SAFE_EXPONENT = 60.0

HEAD_PAD = LANES
COL_CQ = 0
COL_CKV = COL_CQ + MLA_LORA
COL_KR = COL_CKV + MLA_LORA
COL_U = COL_KR + HEAD_PAD
COL_V = COL_U + SGU_DIM
EVEN_COLS = COL_V + SGU_DIM
ROPE_HALF = MLA_ROPE // 2
ROPE_LO = MLA_NOPE


def _params(semantics):
    return pltpu.CompilerParams(dimension_semantics=semantics, vmem_limit_bytes=VMEM_LIMIT_BYTES)


def _resident(shape):
    zeros = (0,) * len(shape)
    return pl.BlockSpec(shape, lambda *_: zeros, pipeline_mode=pl.Buffered(1))


def _dot(a, b):
    return jnp.dot(a, b, preferred_element_type=jnp.float32)


def _dot_nt(a, b):
    return lax.dot_general(a, b, (((1,), (1,)), ((), ())), preferred_element_type=jnp.float32)


def _gelu_tanh(x):
    c = math.sqrt(2.0 / math.pi)
    return 0.5 * x * (1.0 + jnp.tanh(c * (x + 0.044715 * (x * x * x))))


def _layer_norm(x, g, b):
    mu = jnp.mean(x, axis=-1, keepdims=True)
    xc = x - mu
    var = jnp.mean(xc * xc, axis=-1, keepdims=True)
    return xc * lax.rsqrt(var + NORM_EPS) * g + b


def _rms_norm(x, g):
    return x * lax.rsqrt(jnp.mean(x * x, axis=-1, keepdims=True) + NORM_EPS) * g


def _even_front_kernel(x_ref, pos_ref, freq_ref, w_in_ref, gq_ref, gkv_ref, wqt_ref, wk_ref, wvt_ref, one_ref,
                       lng_ref, lnb_ref, sw_ref, sb_ref, qt_ref, k_ref, vt_ref, b_ref):
    chains = [slice(r, r + FRONT_CHAIN_ROWS) for r in range(0, x_ref.shape[0], FRONT_CHAIN_ROWS)]
    zs = [_dot(x_ref[rows, :].astype(jnp.bfloat16), w_in_ref[...]) for rows in chains]
    for rows, z in zip(chains, zs):
        _even_front_chain(rows, z, pos_ref, freq_ref, gq_ref, gkv_ref, wqt_ref, wk_ref, wvt_ref, one_ref,
                          lng_ref, lnb_ref, sw_ref, sb_ref, qt_ref, k_ref, vt_ref, b_ref)


def _even_front_chain(rows, z, pos_ref, freq_ref, gq_ref, gkv_ref, wqt_ref, wk_ref, wvt_ref, one_ref,
                      lng_ref, lnb_ref, sw_ref, sb_ref, qt_ref, k_ref, vt_ref, b_ref):
    tm = z.shape[0]
    rep = tm // LANES

    ang = jnp.tile(freq_ref[...], (1, rep)) * pos_ref[:, rows].astype(jnp.float32)
    cos_t = jnp.cos(ang)
    sin_t = jnp.sin(ang)

    cq = _rms_norm(z[:, COL_CQ:COL_CQ + MLA_LORA], gq_ref[...] * (MLA_SCALE * LOG2_E)).astype(jnp.bfloat16)
    ckv = _rms_norm(z[:, COL_CKV:COL_CKV + MLA_LORA], gkv_ref[...]).astype(jnp.bfloat16)

    qt = _dot_nt(wqt_ref[...], cq)
    for h in range(MLA_HEADS):
        base = h * HEAD_PAD
        x1 = qt[base + ROPE_LO:base + ROPE_LO + ROPE_HALF]
        x2 = qt[base + ROPE_LO + ROPE_HALF:base + ROPE_LO + MLA_ROPE]
        slab = jnp.concatenate([qt[base:base + ROPE_LO], x1 * cos_t - x2 * sin_t, x2 * cos_t + x1 * sin_t,
                                qt[base + ROPE_LO + MLA_ROPE:base + HEAD_PAD]], axis=0)
        qt_ref[base:base + HEAD_PAD, rows] = slab.astype(qt_ref.dtype)

    vt = _dot_nt(wvt_ref[...], ckv) + jnp.tile(one_ref[...], (1, rep))
    vt_ref[:, rows] = vt.astype(vt_ref.dtype)

    ones = jnp.ones((ROPE_LO, tm), jnp.float32)
    zeros = jnp.zeros((HEAD_PAD - ROPE_LO - MLA_ROPE, tm), jnp.float32)
    cos_k = jnp.concatenate([ones, cos_t, cos_t, zeros], axis=0).T
    sin_k = jnp.concatenate([0.0 * ones, -sin_t, sin_t, zeros], axis=0).T
    lane = lax.broadcasted_iota(jnp.int32, (tm, LANES), 1)
    kr = z[:, COL_KR:COL_KR + HEAD_PAD]
    partner = jnp.where(lane < ROPE_LO + ROPE_HALF, pltpu.roll(kr, LANES - ROPE_HALF, 1), pltpu.roll(kr, ROPE_HALF, 1))
    kr = kr * cos_k + partner * sin_k
    kn = _dot(ckv, wk_ref[...])
    for h in range(MLA_HEADS):
        sl = slice(h * HEAD_PAD, (h + 1) * HEAD_PAD)
        k_ref[rows, sl] = (kn[:, sl] + kr).astype(k_ref.dtype)

    u = _gelu_tanh(z[:, COL_U:COL_U + SGU_DIM])
    vn = _layer_norm(_gelu_tanh(z[:, COL_V:COL_V + SGU_DIM]), lng_ref[...], lnb_ref[...]).astype(jnp.bfloat16)
    row = lax.broadcasted_iota(jnp.int32, (SGU_CHUNK, SGU_CHUNK), 0)
    col = lax.broadcasted_iota(jnp.int32, (SGU_CHUNK, SGU_CHUNK), 1)
    causal = col <= row
    bias = sb_ref[...]
    for g in range(SGU_GROUPS):
        wg = jnp.where(causal, sw_ref[g], 0.0).astype(jnp.bfloat16)
        gs = slice(g * SGU_GROUP_DIM, (g + 1) * SGU_GROUP_DIM)
        for c in range(tm // SGU_CHUNK):
            rs = slice(c * SGU_CHUNK, (c + 1) * SGU_CHUNK)
            mixed = _dot(wg, vn[rs, gs]) + bias[:, gs]
            out_rows = slice(rows.start + rs.start, rows.start + rs.stop)
            b_ref[out_rows, gs] = (u[rs, gs] * mixed).astype(b_ref.dtype)


def _even_front(x2, pos_row, freq_col, w_in, gq, gkv, wqt, wk, wvt, one_col, lng, lnb, sw, sb):
    t = x2.shape[0]
    tm = EVEN_TILE
    row = lambda i: (i, 0)
    col = lambda i: (0, i)
    wide = MLA_HEADS * HEAD_PAD
    out_shapes = (
        jax.ShapeDtypeStruct((wide, t), jnp.bfloat16),
        jax.ShapeDtypeStruct((t, wide), jnp.bfloat16),
        jax.ShapeDtypeStruct((wide, t), jnp.bfloat16),
        jax.ShapeDtypeStruct((t, SGU_DIM), jnp.bfloat16),
    )
    consts = (freq_col, w_in, gq, gkv, wqt, wk, wvt, one_col, lng, lnb, sw, sb)
    return pl.pallas_call(
        _even_front_kernel,
        grid=(t // tm,),
        in_specs=[pl.BlockSpec((tm, D_MODEL), row), pl.BlockSpec((1, tm), col)] + [_resident(a.shape) for a in consts],
        out_specs=(
            pl.BlockSpec((wide, tm), col),
            pl.BlockSpec((tm, wide), row),
            pl.BlockSpec((wide, tm), col),
            pl.BlockSpec((tm, SGU_DIM), row),
        ),
        out_shape=out_shapes,
        compiler_params=_params(("parallel",)),
        name="even_front",
    )(x2, pos_row, *consts)


def _attn_kernel(step_ref, qt_ref, k_ref, vt_ref, o_ref, s_sc, cm_sc, m_sc, acc_sc):
    t = ATTN_TILE
    nq = qt_ref.shape[1] // t
    n_off = nq * (nq - 1) // 2
    assert n_off % 2 == 0 and n_off >= 2
    m_sc[...] = jnp.full(m_sc.shape, MASK_VALUE, jnp.float32)
    acc_sc[...] = jnp.zeros(acc_sc.shape, jnp.float32)

    def tile(i):
        return slice(i * t, (i + 1) * t) if isinstance(i, int) else pl.ds(pl.multiple_of(i * t, t), t)

    def scores(qi, j, slot, hh):
        hs = slice(hh * HEAD_PAD, (hh + 1) * HEAD_PAD)
        s = _dot(k_ref[tile(j), hs], qt_ref[hs, tile(qi)])
        s_sc[slot, hh] = s
        cm_sc[slot, hh] = jnp.broadcast_to(jnp.max(s, axis=0, keepdims=True), (SUBLANES, t))

    def update(qi, j, slot, hh):
        hs = slice(hh * HEAD_PAD, (hh + 1) * HEAD_PAD)
        m_prev = m_sc[qi, hh]
        m_next = jnp.maximum(m_prev, cm_sc[slot, hh])
        p = jnp.exp2(s_sc[slot, hh] - m_next[0:1, :]).astype(jnp.bfloat16)
        alpha = jnp.exp2(m_prev - m_next)
        acc_sc[qi, hh] = alpha[0:1, :] * acc_sc[qi, hh] + _dot(vt_ref[hs, tile(j)], p)
        m_sc[qi, hh] = m_next

    half = t // 2
    key = lax.broadcasted_iota(jnp.int32, (half, half), 0)
    qry = lax.broadcasted_iota(jnp.int32, (half, half), 1)
    causal = key <= qry

    def scores_diag(qi, slot, hh):
        hs = slice(hh * HEAD_PAD, (hh + 1) * HEAD_PAD)
        lo, mid_, hi = qi * t, qi * t + half, (qi + 1) * t
        top = _dot(k_ref[lo:mid_, hs], qt_ref[hs, lo:hi])
        bot = jnp.where(causal, _dot(k_ref[mid_:hi, hs], qt_ref[hs, mid_:hi]), MASK_VALUE)
        top_left = jnp.where(causal, top[:, :half], MASK_VALUE)
        s_sc[slot, hh, 0:half, 0:half] = top_left
        s_sc[slot, hh, 0:half, half:t] = top[:, half:]
        s_sc[slot, hh, half:t, half:t] = bot
        cm = jnp.concatenate([jnp.max(top_left, axis=0, keepdims=True),
                              jnp.maximum(jnp.max(top[:, half:], axis=0, keepdims=True),
                                          jnp.max(bot, axis=0, keepdims=True))], axis=1)
        cm_sc[slot, hh] = jnp.broadcast_to(cm, (SUBLANES, t))

    def update_diag(qi, slot, hh):
        hs = slice(hh * HEAD_PAD, (hh + 1) * HEAD_PAD)
        lo, mid_, hi = qi * t, qi * t + half, (qi + 1) * t
        m_prev = m_sc[qi, hh]
        m_next = jnp.maximum(m_prev, cm_sc[slot, hh])
        p_top = jnp.exp2(s_sc[slot, hh, 0:half, :] - m_next[0:1, :]).astype(jnp.bfloat16)
        p_bot = jnp.exp2(s_sc[slot, hh, half:t, half:t] - m_next[0:1, half:]).astype(jnp.bfloat16)
        pv = _dot(vt_ref[hs, lo:mid_], p_top)
        pv = jnp.concatenate([pv[:, :half], pv[:, half:] + _dot(vt_ref[hs, mid_:hi], p_bot)], axis=1)
        return jnp.exp2(m_prev - m_next)[0:1, :] * acc_sc[qi, hh] + pv

    def off_step(n):
        return step_ref[0, n], step_ref[1, n]

    for hh in range(2):
        scores(*off_step(0), 0, hh)

    def off_pair(n):
        for slot in range(2):
            for hh in range(2):
                scores(*off_step(n + slot + 1), 1 - slot, hh)
                update(*off_step(n + slot), slot, hh)

    unroll = ATTN_STEPS_PER_TRIP
    looped = (n_off - 2) // unroll * unroll

    def body(p, carry):
        for i in range(0, unroll, 2):
            off_pair(unroll * p + i)
        return carry

    lax.fori_loop(0, looped // unroll, body, 0)
    for n in range(looped, n_off - 2, 2):
        off_pair(n)
    for hh in range(2):
        scores(*off_step(n_off - 1), 1, hh)
        update(*off_step(n_off - 2), 0, hh)
    for hh in range(2):
        scores_diag(0, 0, hh)
        update(*off_step(n_off - 1), 1, hh)

    lane = lax.broadcasted_iota(jnp.int32, (t, LANES), 1)
    for qi in range(nq):
        slot = qi % 2
        outs = []
        for hh in range(2):
            if qi + 1 < nq:
                scores_diag(qi + 1, 1 - slot, hh)
            acc = update_diag(qi, slot, hh)
            outs.append((acc / acc[MLA_V:MLA_V + 1, :]).T)
        pair = jnp.where(lane < MLA_V, outs[0], pltpu.roll(outs[1], MLA_V, 1))
        o_ref[qi * t:(qi + 1) * t, :] = pair.astype(o_ref.dtype)


def _attention(qt, k, vt, bsz, seq):
    t = ATTN_TILE
    nq = seq // t
    pairs = MLA_HEADS // 2
    steps = jnp.asarray([[qi for qi in range(nq) for _ in range(qi)],
                         [j for qi in range(nq) for j in range(qi)]], jnp.int32)
    grid_spec = pltpu.PrefetchScalarGridSpec(
        num_scalar_prefetch=1,
        grid=(bsz, pairs),
        in_specs=[
            pl.BlockSpec((2 * HEAD_PAD, seq), lambda bi, p, steps_ref: (p, bi)),
            pl.BlockSpec((None, seq, 2 * HEAD_PAD), lambda bi, p, steps_ref: (bi, 0, p)),
            pl.BlockSpec((2 * HEAD_PAD, seq), lambda bi, p, steps_ref: (p, bi)),
        ],
        out_specs=pl.BlockSpec((None, seq, 2 * MLA_V), lambda bi, p, steps_ref: (bi, 0, p)),
        scratch_shapes=[
            pltpu.VMEM((2, 2, t, t), jnp.float32),
            pltpu.VMEM((2, 2, SUBLANES, t), jnp.float32),
            pltpu.VMEM((nq, 2, SUBLANES, t), jnp.float32),
            pltpu.VMEM((nq, 2, HEAD_PAD, t), jnp.float32),
        ],
    )
    return pl.pallas_call(
        _attn_kernel,
        grid_spec=grid_spec,
        out_shape=jax.ShapeDtypeStruct((bsz, seq, MLA_HEADS * MLA_V), jnp.bfloat16),
        compiler_params=_params(("parallel", "parallel")),
        name="mla_attention",
    )(steps, qt, k, vt)


def _post_kernel(n_mix, *refs):
    h_ref = refs[0]
    mix_refs = refs[1:1 + n_mix]
    wout_refs = refs[1 + n_mix:1 + 2 * n_mix]
    g1_ref, b1_ref, w1_ref, w2_ref, g2_ref, b2_ref, o_ref = refs[1 + 2 * n_mix:]
    tm = h_ref.shape[0]
    halves = [slice(r, r + POST_CHAIN_ROWS) for r in range(0, tm, POST_CHAIN_ROWS)]
    mixes = []
    for rows in halves:
        mix = _dot(mix_refs[0][rows, :], wout_refs[0][...])
        for m_ref, w_ref in zip(mix_refs[1:], wout_refs[1:]):
            mix = mix + _dot(m_ref[rows, :], w_ref[...])
        mixes.append(mix)
    for rows, mix in zip(halves, mixes):
        y = _layer_norm(DN_ALPHA * h_ref[rows, :] + mix, g1_ref[...], b1_ref[...])
        yb = y.astype(jnp.bfloat16)
        ff = None
        for c in range(D_FF // FF_CHUNK):
            cs = slice(c * FF_CHUNK, (c + 1) * FF_CHUNK)
            a = jnp.maximum(_dot(yb, w1_ref[:, cs]), 0.0)
            part = _dot((a * a).astype(jnp.bfloat16), w2_ref[cs, :])
            ff = part if ff is None else ff + part
        o_ref[rows, :] = _layer_norm(DN_ALPHA * y + ff, g2_ref[...], b2_ref[...])


def _post(layer, h2, mixes, wouts, g1, b1, w1, w2, g2, b2):
    t = h2.shape[0]
    tm = POST_TILE
    row = lambda i: (i, 0)
    n = len(mixes)
    layer_slab = lambda a: pl.BlockSpec((None,) + a.shape[1:], lambda i: (layer, 0, 0), pipeline_mode=pl.Buffered(1))
    in_specs = [pl.BlockSpec((tm, D_MODEL), row)]
    in_specs += [pl.BlockSpec((tm, m.shape[1]), row) for m in mixes]
    in_specs += [_resident(w.shape) for w in wouts]
    in_specs += [_resident(g1.shape), _resident(b1.shape), layer_slab(w1), layer_slab(w2), _resident(g2.shape),
                 _resident(b2.shape)]
    return pl.pallas_call(
        functools.partial(_post_kernel, n),
        grid=(t // tm,),
        in_specs=in_specs,
        out_specs=pl.BlockSpec((tm, D_MODEL), row),
        out_shape=jax.ShapeDtypeStruct((t, D_MODEL), jnp.float32),
        compiler_params=_params(("parallel",)),
        name="post",
    )(h2, *mixes, *wouts, g1, b1, w1, w2, g2, b2)


HG_SLABS = ("q", "k", "v", "gate")
HG_SLAB = {name: slice(i * HG_DK, (i + 1) * HG_DK) for i, name in enumerate(HG_SLABS)}
HG_WIDE = len(HG_SLABS) * HG_DK


def _odd_front_kernel(x_ref, w_ref, lb_ref, gn_ref, o_ref, lg_ref, risk_ref):
    tm = x_ref.shape[0]
    half = HG_CHUNK // 2
    xb = x_ref[...].astype(jnp.bfloat16)
    lbp = lb_ref[...]
    mx = jnp.max(lbp, axis=0, keepdims=True)
    e = jnp.exp(lbp - mx)
    sm = e / jnp.sum(e, axis=0, keepdims=True)
    lb_all = (sm[0:1, :] + sm[1:2, :]) - sm[0:1, :]
    heads_per_dot = 2
    wide = HG_HEADS * HG_DK
    for h in range(HG_HEADS):
        hs = slice(h * HG_DK, (h + 1) * HG_DK)
        if h % heads_per_dot == 0:
            zs = [_dot(xb, w_ref[:, s * wide + h * HG_DK:s * wide + (h + heads_per_dot) * HG_DK]) for s in range(4)]
        sub = slice((h % heads_per_dot) * HG_DK, (h % heads_per_dot + 1) * HG_DK)
        zq, zf, zi, zg = (zz[:, sub] for zz in zs)
        c_lb = 0.5 * (1.0 - lb_all[:, hs])
        ct = c_lb * jnp.tanh(0.5 * zf)
        gate = (1.0 - c_lb) + ct
        lg = jnp.log2(gate)
        lg_ref[:, hs] = lg
        hq = 0.5 * zq
        hg = 0.5 * zg
        base = h * HG_WIDE
        slabs = {"q": hq + hq * jnp.tanh(hq), "k": c_lb - ct, "v": zi,
                 "gate": gn_ref[:, hs] * (hg + hg * jnp.tanh(hg))}
        for name, val in slabs.items():
            sl = HG_SLAB[name]
            o_ref[:, base + sl.start:base + sl.stop] = val.astype(o_ref.dtype)
        risk = None
        for r0 in range(0, tm, half):
            s = -jnp.sum(lg[r0:r0 + half], axis=0, keepdims=True)
            risk = s if risk is None else jnp.maximum(risk, s)
        risk_ref[h:h + 1, :] = jnp.broadcast_to(jnp.max(risk, axis=1, keepdims=True), (1, LANES))


def _odd_front(x2, w, lb, gn):
    t = x2.shape[0]
    tm = TOKEN_TILE
    return pl.pallas_call(
        _odd_front_kernel,
        grid=(t // tm,),
        in_specs=[pl.BlockSpec((tm, x2.shape[1]), lambda i: (i, 0)), _resident(w.shape), _resident(lb.shape),
                  _resident(gn.shape)],
        out_specs=(pl.BlockSpec((tm, HG_HEADS * HG_WIDE), lambda i: (i, 0)),
                   pl.BlockSpec((tm, HG_HEADS * HG_DK), lambda i: (i, 0)),
                   pl.BlockSpec((None, HG_HEADS, LANES), lambda i: (i, 0, 0))),
        out_shape=(jax.ShapeDtypeStruct((t, HG_HEADS * HG_WIDE), jnp.bfloat16),
                   jax.ShapeDtypeStruct((t, HG_HEADS * HG_DK), jnp.float32),
                   jax.ShapeDtypeStruct((t // tm, HG_HEADS, LANES), jnp.float32)),
        compiler_params=_params(("parallel",)),
        name="odd_front",
    )(x2, w, lb, gn)


def _midpoint_rows(b, half):
    c = b.shape[0]
    if half >= SUBLANES:
        parts = []
        for p in range(c // (2 * half)):
            r = p * 2 * half + half - 1
            parts.append(jnp.broadcast_to(b[r:r + 1, :], (2 * half, LANES)))
        return parts[0] if len(parts) == 1 else jnp.concatenate(parts, axis=0)
    b3 = b.reshape(c // SUBLANES, SUBLANES, LANES)
    sub = lax.broadcasted_iota(jnp.int32, b3.shape, 1)
    out = None
    for p in range(SUBLANES // (2 * half)):
        r = p * 2 * half + half - 1
        cand = jnp.broadcast_to(b3[:, r:r + 1, :], b3.shape)
        out = cand if out is None else jnp.where(sub >= p * 2 * half, cand, out)
    return out.reshape(c, LANES)


def _hgrn_kernel(safe_ref, z_ref, lg_ref, tril_ref, o_ref, st_ref):
    c = HG_CHUNK
    mid = c // 2 - 1
    n_chunks = z_ref.shape[0] // c
    n_levels = int(math.log2(c))
    bi, h, li = pl.program_id(0), pl.program_id(1), pl.program_id(2)
    safe = safe_ref[bi * pl.num_programs(2) + li, h] != 0

    @pl.when(li == 0)
    def _():
        st_ref[...] = jnp.zeros(st_ref.shape, jnp.float32)

    tril = tril_ref[...]

    def slab(rs, name):
        return z_ref[rs, HG_SLAB[name]]

    def cumsum(rs):
        lg = lg_ref[rs, :]
        hi = lg.astype(jnp.bfloat16)
        lo = (lg - hi.astype(jnp.float32)).astype(jnp.bfloat16)
        return _dot(tril, hi) + _dot(tril, lo)

    def load_qk(rs):
        return slab(rs, "q").astype(jnp.float32), slab(rs, "k").astype(jnp.float32)

    t_idx = lax.broadcasted_iota(jnp.int32, (c, c), 0)
    s_idx = lax.broadcasted_iota(jnp.int32, (c, c), 1)

    def state_update(rs, kd, b_last):
        upd = lax.dot_general(slab(rs, "v"), kd, (((0,), (0,)), ((), ())), preferred_element_type=jnp.float32)
        return jnp.exp2(b_last), upd

    def chunk_output(rs, qd, st, attn):
        o = _dot_nt(qd, st.astype(jnp.bfloat16))
        o = o + _dot(attn.astype(jnp.bfloat16), slab(rs, "v"))
        o = o * lax.rsqrt(jnp.mean(o * o, axis=-1, keepdims=True) + NORM_EPS)
        o_ref[rs, :] = (o * slab(rs, "gate").astype(jnp.float32)).astype(o_ref.dtype)

    @pl.when(safe)
    def _():
        causal = s_idx <= t_idx
        chunks = [slice(ci * c, (ci + 1) * c) for ci in range(n_chunks)]
        bs = [cumsum(rs) for rs in chunks]
        scores, decays, upds, qds = [], [], [], []
        for rs, b in zip(chunks, bs):
            q, k = load_qk(rs)
            b_mid, b_last = b[mid:mid + 1, :], b[c - 1:c, :]
            d = b - b_mid
            qe = q * jnp.exp2(d)
            ke = k * jnp.exp2(-d)
            scores.append(_dot_nt(qe.astype(jnp.bfloat16), ke.astype(jnp.bfloat16)))
            decay, upd = state_update(rs, (ke * jnp.exp2(b_last - b_mid)).astype(jnp.bfloat16), b_last)
            decays.append(decay)
            upds.append(upd)
            qds.append((qe * jnp.exp2(b_mid)).astype(jnp.bfloat16))
        states = [st_ref[...]]
        for ci in range(n_chunks):
            states.append(states[-1] * decays[ci] + upds[ci])
        st_ref[...] = states[-1]
        for ci, rs in enumerate(chunks):
            chunk_output(rs, qds[ci], states[ci], jnp.where(causal, scores[ci], 0.0))

    @pl.when(jnp.logical_not(safe))
    def _():
        x = jnp.bitwise_xor(t_idx, s_idx)
        level = jnp.zeros((c, c), jnp.int32)
        for bit in range(n_levels):
            level = level + (x >= (1 << bit)).astype(jnp.int32)
        level = jnp.where(s_idx <= t_idx, level, -1)
        row_id = lax.broadcasted_iota(jnp.int32, (c, LANES), 0)

        def chunk_body(ci, carry):
            rs = pl.ds(pl.multiple_of(ci * c, c), c)
            (q, k), b = load_qk(rs), cumsum(rs)
            b_last = b[c - 1:c, :]
            attn = jnp.where(level == 0, _dot_nt(q.astype(jnp.bfloat16), k.astype(jnp.bfloat16)), 0.0)
            for lv in range(1, n_levels + 1):
                half = 1 << (lv - 1)
                right = jnp.bitwise_and(row_id, half) != 0
                d = b - _midpoint_rows(b, half)
                decay = jnp.exp2(jnp.where(right, d, -d))
                zl = (jnp.where(right, q, k) * decay).astype(jnp.bfloat16)
                attn = jnp.where(level == lv, _dot_nt(zl, zl), attn)
            st = st_ref[...]
            decay, upd = state_update(rs, (k * jnp.exp2(b_last - b)).astype(jnp.bfloat16), b_last)
            chunk_output(rs, (q * jnp.exp2(b)).astype(jnp.bfloat16), st, attn)
            st_ref[...] = st * decay + upd
            return carry

        lax.fori_loop(0, n_chunks, chunk_body, 0)


def _hgrn(safe, z, lg, tril):
    b, s, _ = z.shape
    blk = HG_BLOCK
    grid_spec = pltpu.PrefetchScalarGridSpec(
        num_scalar_prefetch=1,
        grid=(b, HG_HEADS, s // blk),
        in_specs=[
            pl.BlockSpec((None, blk, HG_WIDE), lambda bi, h, li, safe_ref: (bi, li, h)),
            pl.BlockSpec((None, blk, HG_DK), lambda bi, h, li, safe_ref: (bi, li, h)),
            pl.BlockSpec(tril.shape, lambda bi, h, li, safe_ref: (0, 0), pipeline_mode=pl.Buffered(1)),
        ],
        out_specs=pl.BlockSpec((None, blk, HG_DV), lambda bi, h, li, safe_ref: (bi, li, h)),
        scratch_shapes=[pltpu.VMEM((HG_DV, HG_DK), jnp.float32)],
    )
    return pl.pallas_call(
        _hgrn_kernel,
        grid_spec=grid_spec,
        out_shape=jax.ShapeDtypeStruct((b, s, HG_HEADS * HG_DV), jnp.bfloat16),
        compiler_params=_params(("parallel", "parallel", "arbitrary")),
        name="hgrn2",
    )(safe, z, lg, tril)


def kernel(x, positions, w_in_e, mla_gq, mla_gkv, w_qb, w_kvb, sgu_ln_g, sgu_ln_b, sgu_w, sgu_b, w_out_e,
           w_in_o, hg_lb, hg_gnorm, w_out_o, ln1_g, ln1_b, w_ff1, w_ff2, ln2_g, ln2_b):
    bsz, seq, d = x.shape
    assert d == D_MODEL and hg_lb.shape[0] == DEPTH == 2
    assert seq % ATTN_TILE == 0 and seq % HG_BLOCK == 0 and (bsz * seq) % TOKEN_TILE == 0
    t = bsz * seq
    bf = jnp.bfloat16
    f32 = jnp.float32
    row = lambda a: a.reshape(1, -1).astype(f32)

    inv_freq = ROPE_BASE ** (-jnp.arange(ROPE_HALF, dtype=f32) / ROPE_HALF)
    freq_col = jnp.broadcast_to(inv_freq[:, None], (ROPE_HALF, LANES))

    we = w_in_e[0].astype(bf)
    s1, s2, s3 = 2 * MLA_LORA, 2 * MLA_LORA + MLA_ROPE, 2 * MLA_LORA + MLA_ROPE + SGU_DIM
    kr_cols = jnp.pad(we[:, s1:s2], ((0, 0), (ROPE_LO, HEAD_PAD - ROPE_LO - MLA_ROPE)))
    w_in_p = jnp.concatenate([we[:, :s1], kr_cols, we[:, s2:s3], we[:, s3:]], axis=1)
    assert w_in_p.shape[1] == EVEN_COLS
    dq = MLA_NOPE + MLA_ROPE
    wq_p = jnp.pad(w_qb[0].reshape(MLA_LORA, MLA_HEADS, dq), ((0, 0), (0, 0), (0, HEAD_PAD - dq)))
    wqt_p = wq_p.reshape(MLA_LORA, MLA_HEADS * HEAD_PAD).T.astype(bf)
    wkv = w_kvb[0].reshape(MLA_LORA, MLA_HEADS, MLA_NOPE + MLA_V)
    wk_p = jnp.pad(wkv[:, :, :MLA_NOPE], ((0, 0), (0, 0), (0, HEAD_PAD - MLA_NOPE)))
    wk_p = wk_p.reshape(MLA_LORA, MLA_HEADS * HEAD_PAD).astype(bf)
    wv_p = jnp.pad(wkv[:, :, MLA_NOPE:], ((0, 0), (0, 0), (0, HEAD_PAD - MLA_V)))
    wvt_p = wv_p.reshape(MLA_LORA, MLA_HEADS * HEAD_PAD).T.astype(bf)
    one_col = jnp.tile(jnp.zeros((HEAD_PAD,), f32).at[MLA_V].set(1.0), MLA_HEADS)
    one_col = jnp.broadcast_to(one_col[:, None], (MLA_HEADS * HEAD_PAD, LANES))
    sgu_bias = jnp.repeat(sgu_b[0].T, SGU_GROUP_DIM, axis=1).astype(f32)

    x2 = x.reshape(t, D_MODEL)
    qt, k, vt, b_out = _even_front(x2, positions.reshape(1, t), freq_col, w_in_p, row(mla_gq[0]), row(mla_gkv[0]),
                                   wqt_p, wk_p, wvt_p, one_col, row(sgu_ln_g[0]), row(sgu_ln_b[0]),
                                   sgu_w[0].astype(f32), sgu_bias)
    a_out = _attention(qt, k.reshape(bsz, seq, -1), vt, bsz, seq)
    a_rows = MLA_HEADS * MLA_V
    w1_bf, w2_bf = w_ff1.astype(bf), w_ff2.astype(bf)
    h1 = _post(0, x2, [a_out.reshape(t, -1), b_out], [w_out_e[0][:a_rows].astype(bf), w_out_e[0][a_rows:].astype(bf)],
               row(ln1_g[0]), row(ln1_b[0]), w1_bf, w2_bf, row(ln2_g[0]), row(ln2_b[0]))

    z, lg, risk = _odd_front(h1, w_in_o[0].astype(bf), hg_lb.astype(f32), row(hg_gnorm[0]))
    risk = risk[:, :, 0].reshape(t // HG_BLOCK, HG_BLOCK // TOKEN_TILE, HG_HEADS).max(axis=1)
    safe = (risk <= HG_SAFE_EXPONENT * LOG2_E).astype(jnp.int32)
    tril = jnp.tril(jnp.ones((HG_CHUNK, HG_CHUNK), bf))
    o = _hgrn(safe, z.reshape(bsz, seq, -1), lg.reshape(bsz, seq, -1), tril)
    h2 = _post(1, h1, [o.reshape(t, -1)], [w_out_o[0].astype(bf)], row(ln1_g[1]), row(ln1_b[1]), w1_bf, w2_bf,
               row(ln2_g[1]), row(ln2_b[1]))
    return h2.reshape(bsz, seq, D_MODEL)
```

```python
import functools
import math

import jax
import jax.numpy as jnp
from jax import lax
from jax.experimental import pallas as pl
from jax.experimental.pallas import tpu as pltpu

D_MODEL = 1024
DEPTH = 2
MLA_HEADS = 8
MLA_LORA = 256
MLA_NOPE = 64
MLA_ROPE = 32
MLA_V = 64
MLA_SCALE = (MLA_NOPE + MLA_ROPE) ** -0.5
LOG2_E = math.log2(math.e)
ROPE_BASE = 10000.0
SGU_GROUPS = 4
SGU_GROUP_DIM = 128
SGU_DIM = SGU_GROUPS * SGU_GROUP_DIM
SGU_CHUNK = 128
HG_HEADS = 8
HG_DK = 128
HG_DV = 128
D_FF = 4 * D_MODEL
DN_ALPHA = (2 * DEPTH) ** 0.25
NORM_EPS = 1e-5

LANES = 128
SUBLANES = 8
VMEM_LIMIT_BYTES = 56 * 1024 * 1024

TOKEN_TILE = 1024
EVEN_TILE = 1024
FRONT_CHAIN_ROWS = 256
POST_TILE = 1024
POST_CHAIN_ROWS = 256
ATTN_TILE = 512
ATTN_STEPS_PER_TRIP = 8
HG_CHUNK = 128
HG_BLOCK = 4096
FF_CHUNK = 1024
MASK_VALUE = -1e30
HG_SAFE_EXPONENT = 60.0

HEAD_PAD = LANES
COL_CQ = 0
COL_CKV = COL_CQ + MLA_LORA
COL_KR = COL_CKV + MLA_LORA
COL_U = COL_KR + HEAD_PAD
COL_V = COL_U + SGU_DIM
EVEN_COLS = COL_V + SGU_DIM
ROPE_HALF = MLA_ROPE // 2
ROPE_LO = MLA_NOPE


def _params(semantics):
    return pltpu.CompilerParams(dimension_semantics=semantics, vmem_limit_bytes=VMEM_LIMIT_BYTES)


def _resident(shape):
    zeros = (0,) * len(shape)
    return pl.BlockSpec(shape, lambda *_: zeros, pipeline_mode=pl.Buffered(1))


def _dot(a, b):
    return jnp.dot(a, b, preferred_element_type=jnp.float32)


def _dot_nt(a, b):
    return lax.dot_general(a, b, (((1,), (1,)), ((), ())), preferred_element_type=jnp.float32)


def _gelu_tanh(x):
    c = math.sqrt(2.0 / math.pi)
    return 0.5 * x * (1.0 + jnp.tanh(c * (x + 0.044715 * (x * x * x))))


def _layer_norm(x, g, b):
    mu = jnp.mean(x, axis=-1, keepdims=True)
    xc = x - mu
    var = jnp.mean(xc * xc, axis=-1, keepdims=True)
    return xc * lax.rsqrt(var + NORM_EPS) * g + b


def _rms_norm(x, g):
    return x * lax.rsqrt(jnp.mean(x * x, axis=-1, keepdims=True) + NORM_EPS) * g


def _even_front_kernel(x_ref, pos_ref, freq_ref, w_in_ref, gq_ref, gkv_ref, wqt_ref, wk_ref, wvt_ref, one_ref,
                       lng_ref, lnb_ref, sw_ref, sb_ref, qt_ref, k_ref, vt_ref, b_ref):
    chains = [slice(r, r + FRONT_CHAIN_ROWS) for r in range(0, x_ref.shape[0], FRONT_CHAIN_ROWS)]
    zs = [_dot(x_ref[rows, :].astype(jnp.bfloat16), w_in_ref[...]) for rows in chains]
    for rows, z in zip(chains, zs):
        _even_front_chain(rows, z, pos_ref, freq_ref, gq_ref, gkv_ref, wqt_ref, wk_ref, wvt_ref, one_ref,
                          lng_ref, lnb_ref, sw_ref, sb_ref, qt_ref, k_ref, vt_ref, b_ref)


def _even_front_chain(rows, z, pos_ref, freq_ref, gq_ref, gkv_ref, wqt_ref, wk_ref, wvt_ref, one_ref,
                      lng_ref, lnb_ref, sw_ref, sb_ref, qt_ref, k_ref, vt_ref, b_ref):
    tm = z.shape[0]
    rep = tm // LANES

    ang = jnp.tile(freq_ref[...], (1, rep)) * pos_ref[:, rows].astype(jnp.float32)
    cos_t = jnp.cos(ang)
    sin_t = jnp.sin(ang)

    cq = _rms_norm(z[:, COL_CQ:COL_CQ + MLA_LORA], gq_ref[...] * (MLA_SCALE * LOG2_E)).astype(jnp.bfloat16)
    ckv = _rms_norm(z[:, COL_CKV:COL_CKV + MLA_LORA], gkv_ref[...]).astype(jnp.bfloat16)

    qt = _dot_nt(wqt_ref[...], cq)
    for h in range(MLA_HEADS):
        base = h * HEAD_PAD
        x1 = qt[base + ROPE_LO:base + ROPE_LO + ROPE_HALF]
        x2 = qt[base + ROPE_LO + ROPE_HALF:base + ROPE_LO + MLA_ROPE]
        slab = jnp.concatenate([qt[base:base + ROPE_LO], x1 * cos_t - x2 * sin_t, x2 * cos_t + x1 * sin_t,
                                qt[base + ROPE_LO + MLA_ROPE:base + HEAD_PAD]], axis=0)
        qt_ref[base:base + HEAD_PAD, rows] = slab.astype(qt_ref.dtype)

    vt = _dot_nt(wvt_ref[...], ckv) + jnp.tile(one_ref[...], (1, rep))
    vt_ref[:, rows] = vt.astype(vt_ref.dtype)

    ones = jnp.ones((ROPE_LO, tm), jnp.float32)
    zeros = jnp.zeros((HEAD_PAD - ROPE_LO - MLA_ROPE, tm), jnp.float32)
    cos_k = jnp.concatenate([ones, cos_t, cos_t, zeros], axis=0).T
    sin_k = jnp.concatenate([0.0 * ones, -sin_t, sin_t, zeros], axis=0).T
    lane = lax.broadcasted_iota(jnp.int32, (tm, LANES), 1)
    kr = z[:, COL_KR:COL_KR + HEAD_PAD]
    partner = jnp.where(lane < ROPE_LO + ROPE_HALF, pltpu.roll(kr, LANES - ROPE_HALF, 1), pltpu.roll(kr, ROPE_HALF, 1))
    kr = kr * cos_k + partner * sin_k
    kn = _dot(ckv, wk_ref[...])
    for h in range(MLA_HEADS):
        sl = slice(h * HEAD_PAD, (h + 1) * HEAD_PAD)
        k_ref[rows, sl] = (kn[:, sl] + kr).astype(k_ref.dtype)

    u = _gelu_tanh(z[:, COL_U:COL_U + SGU_DIM])
    vn = _layer_norm(_gelu_tanh(z[:, COL_V:COL_V + SGU_DIM]), lng_ref[...], lnb_ref[...]).astype(jnp.bfloat16)
    row = lax.broadcasted_iota(jnp.int32, (SGU_CHUNK, SGU_CHUNK), 0)
    col = lax.broadcasted_iota(jnp.int32, (SGU_CHUNK, SGU_CHUNK), 1)
    causal = col <= row
    bias = sb_ref[...]
    for g in range(SGU_GROUPS):
        wg = jnp.where(causal, sw_ref[g], 0.0).astype(jnp.bfloat16)
        gs = slice(g * SGU_GROUP_DIM, (g + 1) * SGU_GROUP_DIM)
        for c in range(tm // SGU_CHUNK):
            rs = slice(c * SGU_CHUNK, (c + 1) * SGU_CHUNK)
            mixed = _dot(wg, vn[rs, gs]) + bias[:, gs]
            out_rows = slice(rows.start + rs.start, rows.start + rs.stop)
            b_ref[out_rows, gs] = (u[rs, gs] * mixed).astype(b_ref.dtype)


def _even_front(x2, pos_row, freq_col, w_in, gq, gkv, wqt, wk, wvt, one_col, lng, lnb, sw, sb):
    t = x2.shape[0]
    tm = EVEN_TILE
    row = lambda i: (i, 0)
    col = lambda i: (0, i)
    wide = MLA_HEADS * HEAD_PAD
    out_shapes = (
        jax.ShapeDtypeStruct((wide, t), jnp.bfloat16),
        jax.ShapeDtypeStruct((t, wide), jnp.bfloat16),
        jax.ShapeDtypeStruct((wide, t), jnp.bfloat16),
        jax.ShapeDtypeStruct((t, SGU_DIM), jnp.bfloat16),
    )
    consts = (freq_col, w_in, gq, gkv, wqt, wk, wvt, one_col, lng, lnb, sw, sb)
    return pl.pallas_call(
        _even_front_kernel,
        grid=(t // tm,),
        in_specs=[pl.BlockSpec((tm, D_MODEL), row), pl.BlockSpec((1, tm), col)] + [_resident(a.shape) for a in consts],
        out_specs=(
            pl.BlockSpec((wide, tm), col),
            pl.BlockSpec((tm, wide), row),
            pl.BlockSpec((wide, tm), col),
            pl.BlockSpec((tm, SGU_DIM), row),
        ),
        out_shape=out_shapes,
        compiler_params=_params(("parallel",)),
        name="even_front",
    )(x2, pos_row, *consts)


def _attn_kernel(step_ref, qt_ref, k_ref, vt_ref, o_ref, s_sc, cm_sc, m_sc, acc_sc):
    t = ATTN_TILE
    nq = qt_ref.shape[1] // t
    n_off = nq * (nq - 1) // 2
    assert n_off % 2 == 0 and n_off >= 2
    m_sc[...] = jnp.full(m_sc.shape, MASK_VALUE, jnp.float32)
    acc_sc[...] = jnp.zeros(acc_sc.shape, jnp.float32)

    def tile(i):
        return slice(i * t, (i + 1) * t) if isinstance(i, int) else pl.ds(pl.multiple_of(i * t, t), t)

    def scores(qi, j, slot, hh):
        hs = slice(hh * HEAD_PAD, (hh + 1) * HEAD_PAD)
        s = _dot(k_ref[tile(j), hs], qt_ref[hs, tile(qi)])
        s_sc[slot, hh] = s
        cm_sc[slot, hh] = jnp.broadcast_to(jnp.max(s, axis=0, keepdims=True), (SUBLANES, t))

    def update(qi, j, slot, hh):
        hs = slice(hh * HEAD_PAD, (hh + 1) * HEAD_PAD)
        m_prev = m_sc[qi, hh]
        m_next = jnp.maximum(m_prev, cm_sc[slot, hh])
        p = jnp.exp2(s_sc[slot, hh] - m_next[0:1, :]).astype(jnp.bfloat16)
        alpha = jnp.exp2(m_prev - m_next)
        acc_sc[qi, hh] = alpha[0:1, :] * acc_sc[qi, hh] + _dot(vt_ref[hs, tile(j)], p)
        m_sc[qi, hh] = m_next

    half = t // 2
    key = lax.broadcasted_iota(jnp.int32, (half, half), 0)
    qry = lax.broadcasted_iota(jnp.int32, (half, half), 1)
    causal = key <= qry

    def scores_diag(qi, slot, hh):
        hs = slice(hh * HEAD_PAD, (hh + 1) * HEAD_PAD)
        lo, mid_, hi = qi * t, qi * t + half, (qi + 1) * t
        top = _dot(k_ref[lo:mid_, hs], qt_ref[hs, lo:hi])
        bot = jnp.where(causal, _dot(k_ref[mid_:hi, hs], qt_ref[hs, mid_:hi]), MASK_VALUE)
        top_left = jnp.where(causal, top[:, :half], MASK_VALUE)
        s_sc[slot, hh, 0:half, 0:half] = top_left
        s_sc[slot, hh, 0:half, half:t] = top[:, half:]
        s_sc[slot, hh, half:t, half:t] = bot
        cm = jnp.concatenate([jnp.max(top_left, axis=0, keepdims=True),
                              jnp.maximum(jnp.max(top[:, half:], axis=0, keepdims=True),
                                          jnp.max(bot, axis=0, keepdims=True))], axis=1)
        cm_sc[slot, hh] = jnp.broadcast_to(cm, (SUBLANES, t))

    def update_diag(qi, slot, hh):
        hs = slice(hh * HEAD_PAD, (hh + 1) * HEAD_PAD)
        lo, mid_, hi = qi * t, qi * t + half, (qi + 1) * t
        m_prev = m_sc[qi, hh]
        m_next = jnp.maximum(m_prev, cm_sc[slot, hh])
        p_top = jnp.exp2(s_sc[slot, hh, 0:half, :] - m_next[0:1, :]).astype(jnp.bfloat16)
        p_bot = jnp.exp2(s_sc[slot, hh, half:t, half:t] - m_next[0:1, half:]).astype(jnp.bfloat16)
        pv = _dot(vt_ref[hs, lo:mid_], p_top)
        pv = jnp.concatenate([pv[:, :half], pv[:, half:] + _dot(vt_ref[hs, mid_:hi], p_bot)], axis=1)
        return jnp.exp2(m_prev - m_next)[0:1, :] * acc_sc[qi, hh] + pv

    def off_step(n):
        return step_ref[0, n], step_ref[1, n]

    for hh in range(2):
        scores(*off_step(0), 0, hh)

    def off_pair(n):
        for slot in range(2):
            for hh in range(2):
                scores(*off_step(n + slot + 1), 1 - slot, hh)
                update(*off_step(n + slot), slot, hh)

    unroll = ATTN_STEPS_PER_TRIP
    looped = (n_off - 2) // unroll * unroll

    def body(p, carry):
        for i in range(0, unroll, 2):
            off_pair(unroll * p + i)
        return carry

    lax.fori_loop(0, looped // unroll, body, 0)
    for n in range(looped, n_off - 2, 2):
        off_pair(n)
    for hh in range(2):
        scores(*off_step(n_off - 1), 1, hh)
        update(*off_step(n_off - 2), 0, hh)
    for hh in range(2):
        scores_diag(0, 0, hh)
        update(*off_step(n_off - 1), 1, hh)

    lane = lax.broadcasted_iota(jnp.int32, (t, LANES), 1)
    for qi in range(nq):
        slot = qi % 2
        outs = []
        for hh in range(2):
            if qi + 1 < nq:
                scores_diag(qi + 1, 1 - slot, hh)
            acc = update_diag(qi, slot, hh)
            outs.append((acc / acc[MLA_V:MLA_V + 1, :]).T)
        pair = jnp.where(lane < MLA_V, outs[0], pltpu.roll(outs[1], MLA_V, 1))
        o_ref[qi * t:(qi + 1) * t, :] = pair.astype(o_ref.dtype)


def _attention(qt, k, vt, bsz, seq):
    t = ATTN_TILE
    nq = seq // t
    pairs = MLA_HEADS // 2
    steps = jnp.asarray([[qi for qi in range(nq) for _ in range(qi)],
                         [j for qi in range(nq) for j in range(qi)]], jnp.int32)
    grid_spec = pltpu.PrefetchScalarGridSpec(
        num_scalar_prefetch=1,
        grid=(bsz, pairs),
        in_specs=[
            pl.BlockSpec((2 * HEAD_PAD, seq), lambda bi, p, steps_ref: (p, bi)),
            pl.BlockSpec((None, seq, 2 * HEAD_PAD), lambda bi, p, steps_ref: (bi, 0, p)),
            pl.BlockSpec((2 * HEAD_PAD, seq), lambda bi, p, steps_ref: (p, bi)),
        ],
        out_specs=pl.BlockSpec((None, seq, 2 * MLA_V), lambda bi, p, steps_ref: (bi, 0, p)),
        scratch_shapes=[
            pltpu.VMEM((2, 2, t, t), jnp.float32),
            pltpu.VMEM((2, 2, SUBLANES, t), jnp.float32),
            pltpu.VMEM((nq, 2, SUBLANES, t), jnp.float32),
            pltpu.VMEM((nq, 2, HEAD_PAD, t), jnp.float32),
        ],
    )
    return pl.pallas_call(
        _attn_kernel,
        grid_spec=grid_spec,
        out_shape=jax.ShapeDtypeStruct((bsz, seq, MLA_HEADS * MLA_V), jnp.bfloat16),
        compiler_params=_params(("parallel", "parallel")),
        name="mla_attention",
    )(steps, qt, k, vt)


def _post_kernel(n_mix, *refs):
    h_ref = refs[0]
    mix_refs = refs[1:1 + n_mix]
    wout_refs = refs[1 + n_mix:1 + 2 * n_mix]
    g1_ref, b1_ref, w1_ref, w2_ref, g2_ref, b2_ref, o_ref = refs[1 + 2 * n_mix:]
    tm = h_ref.shape[0]
    halves = [slice(r, r + POST_CHAIN_ROWS) for r in range(0, tm, POST_CHAIN_ROWS)]
    mixes = []
    for rows in halves:
        mix = _dot(mix_refs[0][rows, :], wout_refs[0][...])
        for m_ref, w_ref in zip(mix_refs[1:], wout_refs[1:]):
            mix = mix + _dot(m_ref[rows, :], w_ref[...])
        mixes.append(mix)
    for rows, mix in zip(halves, mixes):
        y = _layer_norm(DN_ALPHA * h_ref[rows, :] + mix, g1_ref[...], b1_ref[...])
        yb = y.astype(jnp.bfloat16)
        ff = None
        for c in range(D_FF // FF_CHUNK):
            cs = slice(c * FF_CHUNK, (c + 1) * FF_CHUNK)
            a = jnp.maximum(_dot(yb, w1_ref[:, cs]), 0.0)
            part = _dot((a * a).astype(jnp.bfloat16), w2_ref[cs, :])
            ff = part if ff is None else ff + part
        o_ref[rows, :] = _layer_norm(DN_ALPHA * y + ff, g2_ref[...], b2_ref[...])


def _post(layer, h2, mixes, wouts, g1, b1, w1, w2, g2, b2):
    t = h2.shape[0]
    tm = POST_TILE
    row = lambda i: (i, 0)
    n = len(mixes)
    layer_slab = lambda a: pl.BlockSpec((None,) + a.shape[1:], lambda i: (layer, 0, 0), pipeline_mode=pl.Buffered(1))
    in_specs = [pl.BlockSpec((tm, D_MODEL), row)]
    in_specs += [pl.BlockSpec((tm, m.shape[1]), row) for m in mixes]
    in_specs += [_resident(w.shape) for w in wouts]
    in_specs += [_resident(g1.shape), _resident(b1.shape), layer_slab(w1), layer_slab(w2), _resident(g2.shape),
                 _resident(b2.shape)]
    return pl.pallas_call(
        functools.partial(_post_kernel, n),
        grid=(t // tm,),
        in_specs=in_specs,
        out_specs=pl.BlockSpec((tm, D_MODEL), row),
        out_shape=jax.ShapeDtypeStruct((t, D_MODEL), jnp.float32),
        compiler_params=_params(("parallel",)),
        name="post",
    )(h2, *mixes, *wouts, g1, b1, w1, w2, g2, b2)


HG_SLABS = ("q", "k", "v", "gate")
HG_SLAB = {name: slice(i * HG_DK, (i + 1) * HG_DK) for i, name in enumerate(HG_SLABS)}
HG_WIDE = len(HG_SLABS) * HG_DK


def _odd_front_kernel(x_ref, w_ref, lb_ref, gn_ref, o_ref, lg_ref, risk_ref):
    tm = x_ref.shape[0]
    half = HG_CHUNK // 2
    xb = x_ref[...].astype(jnp.bfloat16)
    lbp = lb_ref[...]
    mx = jnp.max(lbp, axis=0, keepdims=True)
    e = jnp.exp(lbp - mx)
    sm = e / jnp.sum(e, axis=0, keepdims=True)
    lb_all = (sm[0:1, :] + sm[1:2, :]) - sm[0:1, :]
    heads_per_dot = 2
    wide = HG_HEADS * HG_DK
    for h in range(HG_HEADS):
        hs = slice(h * HG_DK, (h + 1) * HG_DK)
        if h % heads_per_dot == 0:
            zs = [_dot(xb, w_ref[:, s * wide + h * HG_DK:s * wide + (h + heads_per_dot) * HG_DK]) for s in range(4)]
        sub = slice((h % heads_per_dot) * HG_DK, (h % heads_per_dot + 1) * HG_DK)
        zq, zf, zi, zg = (zz[:, sub] for zz in zs)
        c_lb = 0.5 * (1.0 - lb_all[:, hs])
        ct = c_lb * jnp.tanh(0.5 * zf)
        gate = (1.0 - c_lb) + ct
        lg = jnp.log2(gate)
        lg_ref[:, hs] = lg
        hq = 0.5 * zq
        hg = 0.5 * zg
        base = h * HG_WIDE
        slabs = {"q": hq + hq * jnp.tanh(hq), "k": c_lb - ct, "v": zi,
                 "gate": gn_ref[:, hs] * (hg + hg * jnp.tanh(hg))}
        for name, val in slabs.items():
            sl = HG_SLAB[name]
            o_ref[:, base + sl.start:base + sl.stop] = val.astype(o_ref.dtype)
        risk = None
        for r0 in range(0, tm, half):
            s = -jnp.sum(lg[r0:r0 + half], axis=0, keepdims=True)
            risk = s if risk is None else jnp.maximum(risk, s)
        risk_ref[h:h + 1, :] = jnp.broadcast_to(jnp.max(risk, axis=1, keepdims=True), (1, LANES))


def _odd_front(x2, w, lb, gn):
    t = x2.shape[0]
    tm = TOKEN_TILE
    return pl.pallas_call(
        _odd_front_kernel,
        grid=(t // tm,),
        in_specs=[pl.BlockSpec((tm, x2.shape[1]), lambda i: (i, 0)), _resident(w.shape), _resident(lb.shape),
                  _resident(gn.shape)],
        out_specs=(pl.BlockSpec((tm, HG_HEADS * HG_WIDE), lambda i: (i, 0)),
                   pl.BlockSpec((tm, HG_HEADS * HG_DK), lambda i: (i, 0)),
                   pl.BlockSpec((None, HG_HEADS, LANES), lambda i: (i, 0, 0))),
        out_shape=(jax.ShapeDtypeStruct((t, HG_HEADS * HG_WIDE), jnp.bfloat16),
                   jax.ShapeDtypeStruct((t, HG_HEADS * HG_DK), jnp.float32),
                   jax.ShapeDtypeStruct((t // tm, HG_HEADS, LANES), jnp.float32)),
        compiler_params=_params(("parallel",)),
        name="odd_front",
    )(x2, w, lb, gn)


def _midpoint_rows(b, half):
    c = b.shape[0]
    if half >= SUBLANES:
        parts = []
        for p in range(c // (2 * half)):
            r = p * 2 * half + half - 1
            parts.append(jnp.broadcast_to(b[r:r + 1, :], (2 * half, LANES)))
        return parts[0] if len(parts) == 1 else jnp.concatenate(parts, axis=0)
    b3 = b.reshape(c // SUBLANES, SUBLANES, LANES)
    sub = lax.broadcasted_iota(jnp.int32, b3.shape, 1)
    out = None
    for p in range(SUBLANES // (2 * half)):
        r = p * 2 * half + half - 1
        cand = jnp.broadcast_to(b3[:, r:r + 1, :], b3.shape)
        out = cand if out is None else jnp.where(sub >= p * 2 * half, cand, out)
    return out.reshape(c, LANES)


def _hgrn_kernel(safe_ref, z_ref, lg_ref, tril_ref, o_ref, st_ref):
    c = HG_CHUNK
    mid = c // 2 - 1
    n_chunks = z_ref.shape[0] // c
    n_levels = int(math.log2(c))
    bi, h, li = pl.program_id(0), pl.program_id(1), pl.program_id(2)
    safe = safe_ref[bi * pl.num_programs(2) + li, h] != 0

    @pl.when(li == 0)
    def _():
        st_ref[...] = jnp.zeros(st_ref.shape, jnp.float32)

    tril = tril_ref[...]

    def slab(rs, name):
        return z_ref[rs, HG_SLAB[name]]

    def cumsum(rs):
        lg = lg_ref[rs, :]
        hi = lg.astype(jnp.bfloat16)
        lo = (lg - hi.astype(jnp.float32)).astype(jnp.bfloat16)
        return _dot(tril, hi) + _dot(tril, lo)

    def load_qk(rs):
        return slab(rs, "q").astype(jnp.float32), slab(rs, "k").astype(jnp.float32)

    t_idx = lax.broadcasted_iota(jnp.int32, (c, c), 0)
    s_idx = lax.broadcasted_iota(jnp.int32, (c, c), 1)

    def state_update(rs, kd, b_last):
        upd = lax.dot_general(slab(rs, "v"), kd, (((0,), (0,)), ((), ())), preferred_element_type=jnp.float32)
        return jnp.exp2(b_last), upd

    def chunk_output(rs, qd, st, attn):
        o = _dot_nt(qd, st.astype(jnp.bfloat16))
        o = o + _dot(attn.astype(jnp.bfloat16), slab(rs, "v"))
        o = o * lax.rsqrt(jnp.mean(o * o, axis=-1, keepdims=True) + NORM_EPS)
        o_ref[rs, :] = (o * slab(rs, "gate").astype(jnp.float32)).astype(o_ref.dtype)

    @pl.when(safe)
    def _():
        causal = s_idx <= t_idx
        chunks = [slice(ci * c, (ci + 1) * c) for ci in range(n_chunks)]
        bs = [cumsum(rs) for rs in chunks]
        scores, decays, upds, qds = [], [], [], []
        for rs, b in zip(chunks, bs):
            q, k = load_qk(rs)
            b_mid, b_last = b[mid:mid + 1, :], b[c - 1:c, :]
            d = b - b_mid
            qe = q * jnp.exp2(d)
            ke = k * jnp.exp2(-d)
            scores.append(_dot_nt(qe.astype(jnp.bfloat16), ke.astype(jnp.bfloat16)))
            decay, upd = state_update(rs, (ke * jnp.exp2(b_last - b_mid)).astype(jnp.bfloat16), b_last)
            decays.append(decay)
            upds.append(upd)
            qds.append((qe * jnp.exp2(b_mid)).astype(jnp.bfloat16))
        states = [st_ref[...]]
        for ci in range(n_chunks):
            states.append(states[-1] * decays[ci] + upds[ci])
        st_ref[...] = states[-1]
        for ci, rs in enumerate(chunks):
            chunk_output(rs, qds[ci], states[ci], jnp.where(causal, scores[ci], 0.0))

    @pl.when(jnp.logical_not(safe))
    def _():
        x = jnp.bitwise_xor(t_idx, s_idx)
        level = jnp.zeros((c, c), jnp.int32)
        for bit in range(n_levels):
            level = level + (x >= (1 << bit)).astype(jnp.int32)
        level = jnp.where(s_idx <= t_idx, level, -1)
        row_id = lax.broadcasted_iota(jnp.int32, (c, LANES), 0)

        def chunk_body(ci, carry):
            rs = pl.ds(pl.multiple_of(ci * c, c), c)
            (q, k), b = load_qk(rs), cumsum(rs)
            b_last = b[c - 1:c, :]
            attn = jnp.where(level == 0, _dot_nt(q.astype(jnp.bfloat16), k.astype(jnp.bfloat16)), 0.0)
            for lv in range(1, n_levels + 1):
                half = 1 << (lv - 1)
                right = jnp.bitwise_and(row_id, half) != 0
                d = b - _midpoint_rows(b, half)
                decay = jnp.exp2(jnp.where(right, d, -d))
                zl = (jnp.where(right, q, k) * decay).astype(jnp.bfloat16)
                attn = jnp.where(level == lv, _dot_nt(zl, zl), attn)
            st = st_ref[...]
            decay, upd = state_update(rs, (k * jnp.exp2(b_last - b)).astype(jnp.bfloat16), b_last)
            chunk_output(rs, (q * jnp.exp2(b)).astype(jnp.bfloat16), st, attn)
            st_ref[...] = st * decay + upd
            return carry

        lax.fori_loop(0, n_chunks, chunk_body, 0)


def _hgrn(safe, z, lg, tril):
    b, s, _ = z.shape
    blk = HG_BLOCK
    grid_spec = pltpu.PrefetchScalarGridSpec(
        num_scalar_prefetch=1,
        grid=(b, HG_HEADS, s // blk),
        in_specs=[
            pl.BlockSpec((None, blk, HG_WIDE), lambda bi, h, li, safe_ref: (bi, li, h)),
            pl.BlockSpec((None, blk, HG_DK), lambda bi, h, li, safe_ref: (bi, li, h)),
            pl.BlockSpec(tril.shape, lambda bi, h, li, safe_ref: (0, 0), pipeline_mode=pl.Buffered(1)),
        ],
        out_specs=pl.BlockSpec((None, blk, HG_DV), lambda bi, h, li, safe_ref: (bi, li, h)),
        scratch_shapes=[pltpu.VMEM((HG_DV, HG_DK), jnp.float32)],
    )
    return pl.pallas_call(
        _hgrn_kernel,
        grid_spec=grid_spec,
        out_shape=jax.ShapeDtypeStruct((b, s, HG_HEADS * HG_DV), jnp.bfloat16),
        compiler_params=_params(("parallel", "parallel", "arbitrary")),
        name="hgrn2",
    )(safe, z, lg, tril)


def kernel(x, positions, w_in_e, mla_gq, mla_gkv, w_qb, w_kvb, sgu_ln_g, sgu_ln_b, sgu_w, sgu_b, w_out_e,
           w_in_o, hg_lb, hg_gnorm, w_out_o, ln1_g, ln1_b, w_ff1, w_ff2, ln2_g, ln2_b):
    bsz, seq, d = x.shape
    assert d == D_MODEL and hg_lb.shape[0] == DEPTH == 2
    assert seq % ATTN_TILE == 0 and seq % HG_BLOCK == 0 and (bsz * seq) % TOKEN_TILE == 0
    t = bsz * seq
    bf = jnp.bfloat16
    f32 = jnp.float32
    row = lambda a: a.reshape(1, -1).astype(f32)

    inv_freq = ROPE_BASE ** (-jnp.arange(ROPE_HALF, dtype=f32) / ROPE_HALF)
    freq_col = jnp.broadcast_to(inv_freq[:, None], (ROPE_HALF, LANES))

    we = w_in_e[0].astype(bf)
    s1, s2, s3 = 2 * MLA_LORA, 2 * MLA_LORA + MLA_ROPE, 2 * MLA_LORA + MLA_ROPE + SGU_DIM
    kr_cols = jnp.pad(we[:, s1:s2], ((0, 0), (ROPE_LO, HEAD_PAD - ROPE_LO - MLA_ROPE)))
    w_in_p = jnp.concatenate([we[:, :s1], kr_cols, we[:, s2:s3], we[:, s3:]], axis=1)
    assert w_in_p.shape[1] == EVEN_COLS
    dq = MLA_NOPE + MLA_ROPE
    wq_p = jnp.pad(w_qb[0].reshape(MLA_LORA, MLA_HEADS, dq), ((0, 0), (0, 0), (0, HEAD_PAD - dq)))
    wqt_p = wq_p.reshape(MLA_LORA, MLA_HEADS * HEAD_PAD).T.astype(bf)
    wkv = w_kvb[0].reshape(MLA_LORA, MLA_HEADS, MLA_NOPE + MLA_V)
    wk_p = jnp.pad(wkv[:, :, :MLA_NOPE], ((0, 0), (0, 0), (0, HEAD_PAD - MLA_NOPE)))
    wk_p = wk_p.reshape(MLA_LORA, MLA_HEADS * HEAD_PAD).astype(bf)
    wv_p = jnp.pad(wkv[:, :, MLA_NOPE:], ((0, 0), (0, 0), (0, HEAD_PAD - MLA_V)))
    wvt_p = wv_p.reshape(MLA_LORA, MLA_HEADS * HEAD_PAD).T.astype(bf)
    one_col = jnp.tile(jnp.zeros((HEAD_PAD,), f32).at[MLA_V].set(1.0), MLA_HEADS)
    one_col = jnp.broadcast_to(one_col[:, None], (MLA_HEADS * HEAD_PAD, LANES))
    sgu_bias = jnp.repeat(sgu_b[0].T, SGU_GROUP_DIM, axis=1).astype(f32)

    x2 = x.reshape(t, D_MODEL)
    qt, k, vt, b_out = _even_front(x2, positions.reshape(1, t), freq_col, w_in_p, row(mla_gq[0]), row(mla_gkv[0]),
                                   wqt_p, wk_p, wvt_p, one_col, row(sgu_ln_g[0]), row(sgu_ln_b[0]),
                                   sgu_w[0].astype(f32), sgu_bias)
    a_out = _attention(qt, k.reshape(bsz, seq, -1), vt, bsz, seq)
    a_rows = MLA_HEADS * MLA_V
    w1_bf, w2_bf = w_ff1.astype(bf), w_ff2.astype(bf)
    h1 = _post(0, x2, [a_out.reshape(t, -1), b_out], [w_out_e[0][:a_rows].astype(bf), w_out_e[0][a_rows:].astype(bf)],
               row(ln1_g[0]), row(ln1_b[0]), w1_bf, w2_bf, row(ln2_g[0]), row(ln2_b[0]))

    z, lg, risk = _odd_front(h1, w_in_o[0].astype(bf), hg_lb.astype(f32), row(hg_gnorm[0]))
    risk = risk[:, :, 0].reshape(t // HG_BLOCK, HG_BLOCK // TOKEN_TILE, HG_HEADS).max(axis=1)
    safe = (risk <= HG_SAFE_EXPONENT * LOG2_E).astype(jnp.int32)
    tril = jnp.tril(jnp.ones((HG_CHUNK, HG_CHUNK), bf))
    o = _hgrn(safe, z.reshape(bsz, seq, -1), lg.reshape(bsz, seq, -1), tril)
    h2 = _post(1, h1, [o.reshape(t, -1)], [w_out_o[0].astype(bf)], row(ln1_g[1]), row(ln1_b[1]), w1_bf, w2_bf,
               row(ln2_g[1]), row(ln2_b[1]))
    return h2.reshape(bsz, seq, D_MODEL)
```
